```python
import jax, jax.numpy as jnp
from jax import lax
import numpy as np

D_MODEL = 1024
BATCH = 4
SEQ = 4096
DEPTH = 1

CHUNK = 64
D_MIX = D_MODEL
D_CONV = D_MIX // 2
CONV_HEADS = 8
CONV_WIDTH = 3
D_POOL = D_MIX - D_CONV
POOL_WINDOWS = (2, 4, 8, 16)
POOL_GROUPS = len(POOL_WINDOWS)
POOL_GC = D_POOL // POOL_GROUPS
D_FF = 2816
EPS = 1e-6

kernel_name = "hybrid_conv_pool_macaron_block"


def rms_norm(x, g):
    xf = x.astype(jnp.float32)
    y = xf * lax.rsqrt(jnp.mean(xf * xf, axis=-1, keepdims=True) + EPS)
    return (y * g.astype(jnp.float32)).astype(x.dtype)


def swiglu(h, w_gate, w_up, w_down):
    return (jax.nn.silu(h @ w_gate) * (h @ w_up)) @ w_down


def causal_depthwise_conv(z, w):
    s = z.shape[1]
    zp = jnp.pad(z, ((0, 0), (CONV_WIDTH - 1, 0), (0, 0)))
    return sum(w[k] * zp[:, k:k + s] for k in range(CONV_WIDTH))


def multiscale_pool_minus_self(u):
    s = u.shape[1]
    uf = u.astype(jnp.float32)
    c = jnp.cumsum(uf, axis=1)
    t1 = jnp.arange(1, s + 1, dtype=jnp.float32)
    outs = []
    for g, w in enumerate(POOL_WINDOWS):
        cg = c[:, :, g]
        shifted = jnp.pad(cg, ((0, 0), (w, 0), (0, 0)))[:, :s]
        count = jnp.minimum(t1, float(w))[None, :, None]
        outs.append((cg - shifted) / count - uf[:, :, g])
    return jnp.stack(outs, axis=2).astype(u.dtype)


def setup_inputs(seed: int = 0) -> dict:
    key = jax.random.key(seed)
    ks = jax.random.split(key, 20)
    f32 = jnp.float32

    def nrm(k, shape, fan_in):
        return jax.random.normal(k, shape, f32) * (fan_in ** -0.5)

    def gain(k, shape):
        return 1.0 + 0.02 * jax.random.normal(k, shape, f32)

    L = DEPTH
    return {
        "x": jax.random.normal(ks[0], (BATCH, SEQ, D_MODEL), f32),
        "norm_ffn1": gain(ks[1], (L, D_MODEL)),
        "ffn1_w_gate": nrm(ks[2], (L, D_MODEL, D_FF), D_MODEL),
        "ffn1_w_up": nrm(ks[3], (L, D_MODEL, D_FF), D_MODEL),
        "ffn1_w_down": nrm(ks[4], (L, D_FF, D_MODEL), D_FF),
        "norm_mix": gain(ks[5], (L, D_MODEL)),
        "w_in": nrm(ks[6], (L, D_MODEL, 3 * D_CONV + D_POOL), D_MODEL),
        "conv_w": nrm(ks[7], (L, CONV_WIDTH, D_CONV), CONV_WIDTH),
        "pool_w": nrm(ks[8], (L, POOL_GROUPS, POOL_GC, POOL_GC), POOL_GC),
        "pool_scale": gain(ks[9], (L, D_POOL)),
        "w_out": nrm(ks[10], (L, D_MIX, D_MODEL), D_MIX),
        "norm_ffn2": gain(ks[11], (L, D_MODEL)),
        "ffn2_w_gate": nrm(ks[12], (L, D_MODEL, D_FF), D_MODEL),
        "ffn2_w_up": nrm(ks[13], (L, D_MODEL, D_FF), D_MODEL),
        "ffn2_w_down": nrm(ks[14], (L, D_FF, D_MODEL), D_FF),
        "norm_final": gain(ks[15], (D_MODEL,)),
    }


def reference(x, norm_ffn1, ffn1_w_gate, ffn1_w_up, ffn1_w_down, norm_mix, w_in, conv_w,
              pool_w, pool_scale, w_out, norm_ffn2, ffn2_w_gate, ffn2_w_up, ffn2_w_down,
              norm_final):
    b, s, _ = x.shape
    for l in range(DEPTH):
        x = x + 0.5 * swiglu(rms_norm(x, norm_ffn1[l]), ffn1_w_gate[l], ffn1_w_up[l], ffn1_w_down[l])

        h = rms_norm(x, norm_mix[l])
        proj = h @ w_in[l]
        v = proj[..., :D_CONV]
        gate_b = proj[..., D_CONV:2 * D_CONV]
        gate_c = proj[..., 2 * D_CONV:3 * D_CONV]
        u = proj[..., 3 * D_CONV:]

        y_a = gate_b * causal_depthwise_conv(gate_c * v, conv_w[l])

        ug = u.reshape(b, s, POOL_GROUPS, POOL_GC)
        pooled = multiscale_pool_minus_self(ug)
        y_b = jnp.einsum("bsgc,gcd->bsgd", pooled, pool_w[l]).reshape(b, s, D_POOL) * pool_scale[l]

        x = x + jnp.concatenate([y_a, y_b], axis=-1) @ w_out[l]

        x = x + 0.5 * swiglu(rms_norm(x, norm_ffn2[l]), ffn2_w_gate[l], ffn2_w_up[l], ffn2_w_down[l])
    return rms_norm(x, norm_final)
```

```python
import functools

import jax
import jax.numpy as jnp
from jax import lax
from jax.experimental import pallas as pl
from jax.experimental.pallas import tpu as pltpu

EPS = 1e-6
CONV_WIDTH = 3
POOL_WINDOWS = (2, 4, 8, 16)

V7X_LANES = 128
V7X_SUBLANES = 8
V7X_VMEM_BYTES = 64 * 1024 * 1024

TOKEN_TILE = 512
CONV_HALO = V7X_SUBLANES
POOL_HALO = 2 * V7X_SUBLANES
VMEM_LIMIT_BYTES = 56 * 1024 * 1024


def _rms_norm(x, gain):
    y = x * lax.rsqrt(jnp.mean(x * x, axis=-1, keepdims=True) + EPS)
    return y * gain


def _ffn_kernel(x_ref, g_ref, wg_ref, wu_ref, wd_ref, gf_ref, o_ref, *, final_norm):
    x = x_ref[...]
    h = _rms_norm(x, g_ref[...]).astype(jnp.bfloat16)
    gate = jnp.dot(h, wg_ref[...], preferred_element_type=jnp.float32)
    up = jnp.dot(h, wu_ref[...], preferred_element_type=jnp.float32)
    act = (gate * jax.nn.sigmoid(gate) * up).astype(jnp.bfloat16)
    y = jnp.dot(act, wd_ref[...], preferred_element_type=jnp.float32)
    out = x + 0.5 * y
    if final_norm:
        out = _rms_norm(out, gf_ref[...])
    o_ref[...] = out


def _resident(shape):
    return pl.BlockSpec(shape, lambda *_: (0,) * len(shape), pipeline_mode=pl.Buffered(1))


def _ffn(x2d, gain, w_gate, w_up, w_down, final_gain, *, final_norm):
    n_tok, d_model = x2d.shape
    d_ff = w_gate.shape[1]
    assert n_tok % TOKEN_TILE == 0
    tile = pl.BlockSpec((TOKEN_TILE, d_model), lambda i: (i, 0))
    return pl.pallas_call(
        functools.partial(_ffn_kernel, final_norm=final_norm),
        grid=(n_tok // TOKEN_TILE,),
        in_specs=[
            tile,
            _resident((1, d_model)),
            _resident((d_model, d_ff)),
            _resident((d_model, d_ff)),
            _resident((d_ff, d_model)),
            _resident((1, d_model)),
        ],
        out_specs=tile,
        out_shape=jax.ShapeDtypeStruct(x2d.shape, x2d.dtype),
        compiler_params=pltpu.CompilerParams(
            dimension_semantics=("arbitrary",), vmem_limit_bytes=VMEM_LIMIT_BYTES),
        name="ffn_final" if final_norm else "ffn",
    )(x2d, gain, w_gate, w_up, w_down, final_gain)


def _mixer_kernel(x_ref, g_ref, win_ref, cw_ref, pw_ref, ps_ref, wout_ref, o_ref,
                  zbuf, ubuf):
    j = pl.program_id(1)
    t_tile = x_ref.shape[1]
    d_conv = zbuf.shape[1]
    d_pool = ubuf.shape[1]
    n_groups = len(POOL_WINDOWS)
    gc = d_pool // n_groups

    @pl.when(j == 0)
    def _():
        zbuf[0:CONV_HALO, :] = jnp.zeros((CONV_HALO, d_conv), jnp.float32)
        ubuf[0:POOL_HALO, :] = jnp.zeros((POOL_HALO, d_pool), jnp.float32)

    x = x_ref[0]
    h = _rms_norm(x, g_ref[...]).astype(jnp.bfloat16)
    proj = jnp.dot(h, win_ref[...], preferred_element_type=jnp.float32)
    v = proj[:, 0:d_conv]
    gate_b = proj[:, d_conv:2 * d_conv]
    gate_c = proj[:, 2 * d_conv:3 * d_conv]
    u = proj[:, 3 * d_conv:]

    zbuf[CONV_HALO:, :] = gate_c * v
    ze = zbuf[...]
    cw = cw_ref[...]
    conv = cw[CONV_WIDTH - 1:CONV_WIDTH, :] * ze
    for k in range(1, CONV_WIDTH):
        conv = conv + cw[CONV_WIDTH - 1 - k:CONV_WIDTH - k, :] * pltpu.roll(ze, k, axis=0)
    y_a = gate_b * conv[CONV_HALO:, :]
    zbuf[0:CONV_HALO, :] = ze[t_tile:, :]

    ubuf[POOL_HALO:, :] = u
    pos = j * t_tile + lax.broadcasted_iota(jnp.int32, (t_tile, gc), 0) + 1
    y_b = []
    for g, w in enumerate(POOL_WINDOWS):
        ue = ubuf[:, g * gc:(g + 1) * gc]
        s = ue
        span = 1
        while span < w:
            s = s + pltpu.roll(s, span, axis=0)
            span *= 2
        count = jnp.minimum(pos, w).astype(jnp.float32)
        pooled = s[POOL_HALO:, :] / count - ue[POOL_HALO:, :]
        yg = jnp.dot(pooled.astype(jnp.bfloat16), pw_ref[g],
                     preferred_element_type=jnp.float32)
        y_b.append(yg * ps_ref[:, g * gc:(g + 1) * gc])
    ubuf[0:POOL_HALO, :] = ubuf[t_tile:, :]

    y = jnp.concatenate([y_a] + y_b, axis=-1).astype(jnp.bfloat16)
    o_ref[0] = x + jnp.dot(y, wout_ref[...], preferred_element_type=jnp.float32)


def _mixer(x, gain, w_in, conv_w, pool_w, pool_scale, w_out):
    b, s, d_model = x.shape
    d_conv = conv_w.shape[1]
    d_pool = pool_scale.shape[1]
    assert s % TOKEN_TILE == 0
    tile = pl.BlockSpec((1, TOKEN_TILE, d_model), lambda i, j: (i, j, 0))
    return pl.pallas_call(
        _mixer_kernel,
        grid=(b, s // TOKEN_TILE),
        in_specs=[
            tile,
            _resident((1, d_model)),
            _resident(w_in.shape),
            _resident(conv_w.shape),
            _resident(pool_w.shape),
            _resident(pool_scale.shape),
            _resident(w_out.shape),
        ],
        out_specs=tile,
        out_shape=jax.ShapeDtypeStruct(x.shape, x.dtype),
        scratch_shapes=[
            pltpu.VMEM((CONV_HALO + TOKEN_TILE, d_conv), jnp.float32),
            pltpu.VMEM((POOL_HALO + TOKEN_TILE, d_pool), jnp.float32),
        ],
        compiler_params=pltpu.CompilerParams(
            dimension_semantics=("arbitrary", "arbitrary"),
            vmem_limit_bytes=VMEM_LIMIT_BYTES),
        name="mixer",
    )(x, gain, w_in, conv_w, pool_w, pool_scale, w_out)


def kernel(x, norm_ffn1, ffn1_w_gate, ffn1_w_up, ffn1_w_down, norm_mix, w_in, conv_w,
           pool_w, pool_scale, w_out, norm_ffn2, ffn2_w_gate, ffn2_w_up, ffn2_w_down,
           norm_final):
    b, s, d_model = x.shape
    depth = norm_ffn1.shape[0]
    assert depth >= 1
    bf16 = jnp.bfloat16
    final_gain = norm_final.reshape(1, d_model)
    for l in range(depth):
        last = l == depth - 1
        x = _ffn(x.reshape(b * s, d_model), norm_ffn1[l].reshape(1, d_model),
                 ffn1_w_gate[l].astype(bf16), ffn1_w_up[l].astype(bf16),
                 ffn1_w_down[l].astype(bf16), final_gain,
                 final_norm=False).reshape(b, s, d_model)
        x = _mixer(x, norm_mix[l].reshape(1, d_model), w_in[l].astype(bf16), conv_w[l],
                   pool_w[l].astype(bf16), pool_scale[l].reshape(1, -1),
                   w_out[l].astype(bf16))
        x = _ffn(x.reshape(b * s, d_model), norm_ffn2[l].reshape(1, d_model),
                 ffn2_w_gate[l].astype(bf16), ffn2_w_up[l].astype(bf16),
                 ffn2_w_down[l].astype(bf16), final_gain,
                 final_norm=last).reshape(b, s, d_model)
    return x
```

```python
import functools

import jax
import jax.numpy as jnp
from jax import lax
from jax.experimental import pallas as pl
from jax.experimental.pallas import tpu as pltpu

EPS = 1e-6
CONV_WIDTH = 3
POOL_WINDOWS = (2, 4, 8, 16)

V7X_SUBLANES = 8
V7X_BF16_ROWS = 16

TOKEN_TILE = 512
CONV_HALO = V7X_SUBLANES
POOL_HALO = 2 * V7X_SUBLANES
LOAD_CHUNKS = 8
VMEM_LIMIT_BYTES = 56 * 1024 * 1024


def _rms_norm(x, gain):
    y = x * lax.rsqrt(jnp.mean(x * x, axis=-1, keepdims=True) + EPS)
    return y * gain


def _resident(shape):
    return pl.BlockSpec(shape, lambda *_: (0,) * len(shape), pipeline_mode=pl.Buffered(1))


def _side_specs(weights, grid):
    n_steps = 1
    for g in grid:
        n_steps *= g
    specs, shapes = [], []
    for w in weights:
        rows, cols = w.shape
        hold = 1
        while (rows * hold) % n_steps or (rows * hold // n_steps) % V7X_BF16_ROWS:
            hold *= 2
            assert hold <= n_steps, (rows, n_steps)

        def index_map(*idx, hold=hold):
            step = idx[0]
            for g, i in zip(grid[1:], idx[1:]):
                step = step * g + i
            return (step // hold, 0)

        specs.append(pl.BlockSpec((rows * hold // n_steps, cols), index_map))
        shapes.append(jax.ShapeDtypeStruct(w.shape, jnp.bfloat16))
    return specs, shapes


def _side_cast(src_refs, dst_refs):
    for src, dst in zip(src_refs, dst_refs):
        dst[...] = src[...].astype(jnp.bfloat16)


def _swiglu_residual(x, gain, wg_ref, wu_ref, wd_ref, final_gain, final_norm):
    h = _rms_norm(x, gain).astype(jnp.bfloat16)
    gate = jnp.dot(h, wg_ref[...], preferred_element_type=jnp.float32)
    up = jnp.dot(h, wu_ref[...], preferred_element_type=jnp.float32)
    act = (gate * jax.nn.sigmoid(gate) * up).astype(jnp.bfloat16)
    y = jnp.dot(act, wd_ref[...], preferred_element_type=jnp.float32)
    out = x + 0.5 * y
    if final_norm:
        out = _rms_norm(out, final_gain)
    return out


def _load_cast(w_hbm, w_vmem, stage, sem):
    rows = w_hbm.shape[0] // LOAD_CHUNKS

    def copy(c):
        return pltpu.make_async_copy(
            w_hbm.at[pl.ds(c * rows, rows)], stage.at[c % 2], sem.at[c % 2])

    copy(0).start()
    for c in range(LOAD_CHUNKS):
        if c + 1 < LOAD_CHUNKS:
            copy(c + 1).start()
        copy(c).wait()
        w_vmem[pl.ds(c * rows, rows), :] = stage[c % 2].astype(jnp.bfloat16)


def _ffn_kernel(*refs, n_side, final_norm, cast_own):
    x_ref, g_ref, wg_ref, wu_ref, wd_ref, gf_ref = refs[:6]
    side_in = refs[6:6 + n_side]
    o_ref = refs[6 + n_side]
    side_out = refs[7 + n_side:7 + 2 * n_side]
    scratch = refs[7 + 2 * n_side:]

    if cast_own:
        wg_bf, wu_bf, wd_bf, stage_in, stage_out, sem = scratch

        @pl.when(pl.program_id(0) == 0)
        def _():
            _load_cast(wg_ref, wg_bf, stage_in, sem)
            _load_cast(wu_ref, wu_bf, stage_in, sem)
            _load_cast(wd_ref, wd_bf, stage_out, sem)

        wg_ref, wu_ref, wd_ref = wg_bf, wu_bf, wd_bf

    _side_cast(side_in, side_out)
    o_ref[...] = _swiglu_residual(x_ref[...], g_ref[...], wg_ref, wu_ref, wd_ref,
                                  gf_ref[...], final_norm)


def _ffn(x2d, gain, w_gate, w_up, w_down, final_gain, side, *, final_norm):
    n_tok, d_model = x2d.shape
    d_ff = w_gate.shape[1]
    assert n_tok % TOKEN_TILE == 0
    grid = (n_tok // TOKEN_TILE,)
    cast_own = w_gate.dtype == jnp.float32
    assert w_up.dtype == w_gate.dtype and w_down.dtype == w_gate.dtype
    tile = pl.BlockSpec((TOKEN_TILE, d_model), lambda i: (i, 0))
    side_specs, side_shapes = _side_specs(side, grid)
    if cast_own:
        assert d_model % (LOAD_CHUNKS * V7X_BF16_ROWS) == 0
        assert d_ff % (LOAD_CHUNKS * V7X_BF16_ROWS) == 0
        hbm = pl.BlockSpec(memory_space=pl.ANY)
        w_specs = [hbm, hbm, hbm]
        scratch = [
            pltpu.VMEM((d_model, d_ff), jnp.bfloat16),
            pltpu.VMEM((d_model, d_ff), jnp.bfloat16),
            pltpu.VMEM((d_ff, d_model), jnp.bfloat16),
            pltpu.VMEM((2, d_model // LOAD_CHUNKS, d_ff), jnp.float32),
            pltpu.VMEM((2, d_ff // LOAD_CHUNKS, d_model), jnp.float32),
            pltpu.SemaphoreType.DMA((2,)),
        ]
    else:
        w_specs = [_resident(w_gate.shape), _resident(w_up.shape), _resident(w_down.shape)]
        scratch = []
    outs = pl.pallas_call(
        functools.partial(_ffn_kernel, n_side=len(side), final_norm=final_norm,
                          cast_own=cast_own),
        grid=grid,
        in_specs=[tile, _resident((1, d_model))] + w_specs + [_resident((1, d_model))]
        + side_specs,
        out_specs=[tile] + side_specs,
        out_shape=[jax.ShapeDtypeStruct(x2d.shape, x2d.dtype)] + side_shapes,
        scratch_shapes=scratch,
        compiler_params=pltpu.CompilerParams(
            dimension_semantics=("arbitrary",), vmem_limit_bytes=VMEM_LIMIT_BYTES),
        name="ffn_final" if final_norm else "ffn",
    )(x2d, gain, w_gate, w_up, w_down, final_gain, *side)
    return outs[0], list(outs[1:])


def _mixer_kernel(*refs, n_side):
    x_ref, g_ref, win_ref, cw_ref, pw_ref, ps_ref, wout_ref = refs[:7]
    side_in = refs[7:7 + n_side]
    o_ref = refs[7 + n_side]
    side_out = refs[8 + n_side:8 + 2 * n_side]
    zbuf, ubuf = refs[8 + 2 * n_side:]
    j = pl.program_id(1)
    t_tile = x_ref.shape[1]
    d_conv = zbuf.shape[1]
    d_pool = ubuf.shape[1]
    n_groups = len(POOL_WINDOWS)
    gc = d_pool // n_groups

    @pl.when(j == 0)
    def _():
        zbuf[0:CONV_HALO, :] = jnp.zeros((CONV_HALO, d_conv), jnp.float32)
        ubuf[0:POOL_HALO, :] = jnp.zeros((POOL_HALO, d_pool), jnp.float32)

    _side_cast(side_in, side_out)

    x = x_ref[0]
    h = _rms_norm(x, g_ref[...]).astype(jnp.bfloat16)
    proj = jnp.dot(h, win_ref[...], preferred_element_type=jnp.float32)
    v = proj[:, 0:d_conv]
    gate_b = proj[:, d_conv:2 * d_conv]
    gate_c = proj[:, 2 * d_conv:3 * d_conv]
    u = proj[:, 3 * d_conv:]

    zbuf[CONV_HALO:, :] = gate_c * v
    ze = zbuf[...]
    cw = cw_ref[...]
    conv = cw[CONV_WIDTH - 1:CONV_WIDTH, :] * ze
    for k in range(1, CONV_WIDTH):
        conv = conv + cw[CONV_WIDTH - 1 - k:CONV_WIDTH - k, :] * pltpu.roll(ze, k, axis=0)
    y_a = gate_b * conv[CONV_HALO:, :]
    zbuf[0:CONV_HALO, :] = ze[t_tile:, :]

    ubuf[POOL_HALO:, :] = u
    pos = j * t_tile + lax.broadcasted_iota(jnp.int32, (t_tile, gc), 0) + 1
    y_b = []
    for g, w in enumerate(POOL_WINDOWS):
        ue = ubuf[:, g * gc:(g + 1) * gc]
        s = ue
        span = 1
        while span < w:
            s = s + pltpu.roll(s, span, axis=0)
            span *= 2
        count = jnp.minimum(pos, w).astype(jnp.float32)
        pooled = s[POOL_HALO:, :] / count - ue[POOL_HALO:, :]
        yg = jnp.dot(pooled.astype(jnp.bfloat16), pw_ref[g * gc:(g + 1) * gc, :],
                     preferred_element_type=jnp.float32)
        y_b.append(yg * ps_ref[:, g * gc:(g + 1) * gc])
    ubuf[0:POOL_HALO, :] = ubuf[t_tile:, :]

    y = jnp.concatenate([y_a] + y_b, axis=-1).astype(jnp.bfloat16)
    o_ref[0] = x + jnp.dot(y, wout_ref[...], preferred_element_type=jnp.float32)


def _mixer(x, gain, w_in, conv_w, pool_w2d, pool_scale, w_out, side):
    b, s, d_model = x.shape
    d_conv = conv_w.shape[1]
    d_pool = pool_scale.shape[1]
    assert s % TOKEN_TILE == 0
    grid = (b, s // TOKEN_TILE)
    tile = pl.BlockSpec((1, TOKEN_TILE, d_model), lambda i, j: (i, j, 0))
    side_specs, side_shapes = _side_specs(side, grid)
    outs = pl.pallas_call(
        functools.partial(_mixer_kernel, n_side=len(side)),
        grid=grid,
        in_specs=[
            tile,
            _resident((1, d_model)),
            _resident(w_in.shape),
            _resident(conv_w.shape),
            _resident(pool_w2d.shape),
            _resident(pool_scale.shape),
            _resident(w_out.shape),
        ] + side_specs,
        out_specs=[tile] + side_specs,
        out_shape=[jax.ShapeDtypeStruct(x.shape, x.dtype)] + side_shapes,
        scratch_shapes=[
            pltpu.VMEM((CONV_HALO + TOKEN_TILE, d_conv), jnp.float32),
            pltpu.VMEM((POOL_HALO + TOKEN_TILE, d_pool), jnp.float32),
        ],
        compiler_params=pltpu.CompilerParams(
            dimension_semantics=("arbitrary", "arbitrary"),
            vmem_limit_bytes=VMEM_LIMIT_BYTES),
        name="mixer",
    )(x, gain, w_in, conv_w, pool_w2d, pool_scale, w_out, *side)
    return outs[0], list(outs[1:])


def kernel(x, norm_ffn1, ffn1_w_gate, ffn1_w_up, ffn1_w_down, norm_mix, w_in, conv_w,
           pool_w, pool_scale, w_out, norm_ffn2, ffn2_w_gate, ffn2_w_up, ffn2_w_down,
           norm_final):
    b, s, d_model = x.shape
    depth = norm_ffn1.shape[0]
    assert depth >= 1
    final_gain = norm_final.reshape(1, d_model)
    gc = pool_w.shape[-1]
    ffn1_w = [ffn1_w_gate[0], ffn1_w_up[0], ffn1_w_down[0]]
    for l in range(depth):
        last = l == depth - 1
        mix_w = [w_in[l], pool_w[l].reshape(-1, gc), w_out[l]]
        ffn2_w = [ffn2_w_gate[l], ffn2_w_up[l], ffn2_w_down[l]]
        next_w = [] if last else [ffn1_w_gate[l + 1], ffn1_w_up[l + 1], ffn1_w_down[l + 1]]

        x2d, mix_w = _ffn(x.reshape(b * s, d_model), norm_ffn1[l].reshape(1, d_model),
                          *ffn1_w, final_gain, mix_w, final_norm=False)
        x, ffn2_w = _mixer(x2d.reshape(b, s, d_model), norm_mix[l].reshape(1, d_model),
                           mix_w[0], conv_w[l], mix_w[1], pool_scale[l].reshape(1, -1),
                           mix_w[2], ffn2_w)
        x2d, ffn1_w = _ffn(x.reshape(b * s, d_model), norm_ffn2[l].reshape(1, d_model),
                           *ffn2_w, final_gain, next_w, final_norm=last)
        x = x2d.reshape(b, s, d_model)
    return x
```

```python
import functools

import jax
import jax.numpy as jnp
from jax import lax
from jax.experimental import pallas as pl
from jax.experimental.pallas import tpu as pltpu

EPS = 1e-6
CONV_WIDTH = 3
POOL_WINDOWS = (2, 4, 8, 16)

V7X_SUBLANES = 8
V7X_BF16_ROWS = 16

FFN_TOKEN_TILE = 512
MIXER_TOKEN_TILE = 1024
CONV_HALO = V7X_SUBLANES
POOL_HALO = 2 * V7X_SUBLANES
LOAD_CHUNKS = 16
LOAD_DEPTH = 4
VMEM_LIMIT_BYTES = 56 * 1024 * 1024


def _rms_norm(x, gain):
    y = x * lax.rsqrt(jnp.mean(x * x, axis=-1, keepdims=True) + EPS)
    return y * gain


def _resident(shape):
    return pl.BlockSpec(shape, lambda *_: (0,) * len(shape), pipeline_mode=pl.Buffered(1))


def _side_specs(weights, grid):
    n_steps = 1
    for g in grid:
        n_steps *= g
    specs, shapes = [], []
    for w in weights:
        rows, cols = w.shape
        hold = 1
        while (rows * hold) % n_steps or (rows * hold // n_steps) % V7X_BF16_ROWS:
            hold *= 2
            assert hold <= n_steps, (rows, n_steps)

        def index_map(*idx, hold=hold):
            step = idx[0]
            for g, i in zip(grid[1:], idx[1:]):
                step = step * g + i
            return (step // hold, 0)

        specs.append(pl.BlockSpec((rows * hold // n_steps, cols), index_map))
        shapes.append(jax.ShapeDtypeStruct(w.shape, jnp.bfloat16))
    return specs, shapes


def _side_cast(src_refs, dst_refs):
    for src, dst in zip(src_refs, dst_refs):
        dst[...] = src[...].astype(jnp.bfloat16)


def _swiglu_residual(x, gain, wg_ref, wu_ref, wd_ref, final_gain, final_norm):
    h = _rms_norm(x, gain).astype(jnp.bfloat16)
    gate = jnp.dot(h, wg_ref[...], preferred_element_type=jnp.float32)
    up = jnp.dot(h, wu_ref[...], preferred_element_type=jnp.float32)
    act = (gate * jax.nn.sigmoid(gate) * up).astype(jnp.bfloat16)
    y = jnp.dot(act, wd_ref[...], preferred_element_type=jnp.float32)
    out = x + 0.5 * y
    if final_norm:
        out = _rms_norm(out, final_gain)
    return out


def _load_cast(jobs):
    chunks = []
    ring_pos = {}
    for w_hbm, w_vmem, stage, sem in jobs:
        rows = stage.shape[1]
        for c in range(w_hbm.shape[0] // rows):
            slot = ring_pos.get(id(stage), 0)
            ring_pos[id(stage)] = (slot + 1) % LOAD_DEPTH
            chunks.append((w_hbm, w_vmem, stage, sem, c * rows, rows, slot))

    def copy(k):
        w_hbm, _, stage, sem, row0, rows, slot = chunks[k]
        return pltpu.make_async_copy(w_hbm.at[pl.ds(row0, rows)], stage.at[slot], sem.at[slot])

    for k in range(min(LOAD_DEPTH, len(chunks))):
        copy(k).start()
    for k, (_, w_vmem, stage, _, row0, rows, slot) in enumerate(chunks):
        copy(k).wait()
        w_vmem[pl.ds(row0, rows), :] = stage[slot].astype(jnp.bfloat16)
        if k + LOAD_DEPTH < len(chunks):
            copy(k + LOAD_DEPTH).start()


def _ffn_kernel(*refs, n_side, final_norm, cast_own):
    x_ref, g_ref, wg_ref, wu_ref, wd_ref, gf_ref = refs[:6]
    side_in = refs[6:6 + n_side]
    o_ref = refs[6 + n_side]
    side_out = refs[7 + n_side:7 + 2 * n_side]
    scratch = refs[7 + 2 * n_side:]

    if cast_own:
        wg_bf, wu_bf, wd_bf, stage_in, stage_out, sem_in, sem_out = scratch

        @pl.when(pl.program_id(0) == 0)
        def _():
            _load_cast([(wg_ref, wg_bf, stage_in, sem_in),
                        (wu_ref, wu_bf, stage_in, sem_in),
                        (wd_ref, wd_bf, stage_out, sem_out)])

        wg_ref, wu_ref, wd_ref = wg_bf, wu_bf, wd_bf

    _side_cast(side_in, side_out)
    o_ref[...] = _swiglu_residual(x_ref[...], g_ref[...], wg_ref, wu_ref, wd_ref,
                                  gf_ref[...], final_norm)


def _ffn(x2d, gain, w_gate, w_up, w_down, final_gain, side, *, final_norm):
    n_tok, d_model = x2d.shape
    d_ff = w_gate.shape[1]
    assert n_tok % FFN_TOKEN_TILE == 0
    grid = (n_tok // FFN_TOKEN_TILE,)
    cast_own = w_gate.dtype == jnp.float32
    assert w_up.dtype == w_gate.dtype and w_down.dtype == w_gate.dtype
    tile = pl.BlockSpec((FFN_TOKEN_TILE, d_model), lambda i: (i, 0))
    side_specs, side_shapes = _side_specs(side, grid)
    if cast_own:
        assert d_model % (LOAD_CHUNKS * V7X_BF16_ROWS) == 0
        assert d_ff % (LOAD_CHUNKS * V7X_BF16_ROWS) == 0
        hbm = pl.BlockSpec(memory_space=pl.ANY)
        w_specs = [hbm, hbm, hbm]
        scratch = [
            pltpu.VMEM((d_model, d_ff), jnp.bfloat16),
            pltpu.VMEM((d_model, d_ff), jnp.bfloat16),
            pltpu.VMEM((d_ff, d_model), jnp.bfloat16),
            pltpu.VMEM((LOAD_DEPTH, d_model // LOAD_CHUNKS, d_ff), jnp.float32),
            pltpu.VMEM((LOAD_DEPTH, d_ff // LOAD_CHUNKS, d_model), jnp.float32),
            pltpu.SemaphoreType.DMA((LOAD_DEPTH,)),
            pltpu.SemaphoreType.DMA((LOAD_DEPTH,)),
        ]
    else:
        w_specs = [_resident(w_gate.shape), _resident(w_up.shape), _resident(w_down.shape)]
        scratch = []
    outs = pl.pallas_call(
        functools.partial(_ffn_kernel, n_side=len(side), final_norm=final_norm,
                          cast_own=cast_own),
        grid=grid,
        in_specs=[tile, _resident((1, d_model))] + w_specs + [_resident((1, d_model))]
        + side_specs,
        out_specs=[tile] + side_specs,
        out_shape=[jax.ShapeDtypeStruct(x2d.shape, x2d.dtype)] + side_shapes,
        scratch_shapes=scratch,
        compiler_params=pltpu.CompilerParams(
            dimension_semantics=("arbitrary",), vmem_limit_bytes=VMEM_LIMIT_BYTES),
        name="ffn_final" if final_norm else "ffn",
    )(x2d, gain, w_gate, w_up, w_down, final_gain, *side)
    return outs[0], list(outs[1:])


def _mixer_kernel(*refs, n_side):
    x_ref, g_ref, win_ref, cw_ref, pw_ref, ps_ref, wout_ref = refs[:7]
    side_in = refs[7:7 + n_side]
    o_ref = refs[7 + n_side]
    side_out = refs[8 + n_side:8 + 2 * n_side]
    zbuf, ubuf = refs[8 + 2 * n_side:]
    j = pl.program_id(1)
    t_tile = x_ref.shape[1]
    d_conv = zbuf.shape[1]
    d_pool = ubuf.shape[1]
    n_groups = len(POOL_WINDOWS)
    gc = d_pool // n_groups

    @pl.when(j == 0)
    def _():
        zbuf[0:CONV_HALO, :] = jnp.zeros((CONV_HALO, d_conv), jnp.float32)
        ubuf[0:POOL_HALO, :] = jnp.zeros((POOL_HALO, d_pool), jnp.float32)

    _side_cast(side_in, side_out)

    x = x_ref[0]
    h = _rms_norm(x, g_ref[...]).astype(jnp.bfloat16)
    proj = jnp.dot(h, win_ref[...], preferred_element_type=jnp.float32)
    v = proj[:, 0:d_conv]
    gate_b = proj[:, d_conv:2 * d_conv]
    gate_c = proj[:, 2 * d_conv:3 * d_conv]
    u = proj[:, 3 * d_conv:]

    zbuf[CONV_HALO:, :] = gate_c * v
    ze = zbuf[...]
    cw = cw_ref[...]
    conv = cw[CONV_WIDTH - 1:CONV_WIDTH, :] * ze
    for k in range(1, CONV_WIDTH):
        conv = conv + cw[CONV_WIDTH - 1 - k:CONV_WIDTH - k, :] * pltpu.roll(ze, k, axis=0)
    y_a = gate_b * conv[CONV_HALO:, :]
    zbuf[0:CONV_HALO, :] = ze[t_tile:, :]

    ubuf[POOL_HALO:, :] = u
    pos = j * t_tile + lax.broadcasted_iota(jnp.int32, (t_tile, gc), 0) + 1
    y_b = []
    for g, w in enumerate(POOL_WINDOWS):
        ue = ubuf[:, g * gc:(g + 1) * gc]
        s = ue
        span = 1
        while span < w:
            s = s + pltpu.roll(s, span, axis=0)
            span *= 2
        count = jnp.minimum(pos, w).astype(jnp.float32)
        pooled = s[POOL_HALO:, :] / count - ue[POOL_HALO:, :]
        yg = jnp.dot(pooled.astype(jnp.bfloat16), pw_ref[g * gc:(g + 1) * gc, :],
                     preferred_element_type=jnp.float32)
        y_b.append(yg * ps_ref[:, g * gc:(g + 1) * gc])
    ubuf[0:POOL_HALO, :] = ubuf[t_tile:, :]

    y = jnp.concatenate([y_a] + y_b, axis=-1).astype(jnp.bfloat16)
    o_ref[0] = x + jnp.dot(y, wout_ref[...], preferred_element_type=jnp.float32)


def _mixer(x, gain, w_in, conv_w, pool_w2d, pool_scale, w_out, side):
    b, s, d_model = x.shape
    d_conv = conv_w.shape[1]
    d_pool = pool_scale.shape[1]
    assert s % MIXER_TOKEN_TILE == 0
    grid = (b, s // MIXER_TOKEN_TILE)
    tile = pl.BlockSpec((1, MIXER_TOKEN_TILE, d_model), lambda i, j: (i, j, 0))
    side_specs, side_shapes = _side_specs(side, grid)
    outs = pl.pallas_call(
        functools.partial(_mixer_kernel, n_side=len(side)),
        grid=grid,
        in_specs=[
            tile,
            _resident((1, d_model)),
            _resident(w_in.shape),
            _resident(conv_w.shape),
            _resident(pool_w2d.shape),
            _resident(pool_scale.shape),
            _resident(w_out.shape),
        ] + side_specs,
        out_specs=[tile] + side_specs,
        out_shape=[jax.ShapeDtypeStruct(x.shape, x.dtype)] + side_shapes,
        scratch_shapes=[
            pltpu.VMEM((CONV_HALO + MIXER_TOKEN_TILE, d_conv), jnp.float32),
            pltpu.VMEM((POOL_HALO + MIXER_TOKEN_TILE, d_pool), jnp.float32),
        ],
        compiler_params=pltpu.CompilerParams(
            dimension_semantics=("arbitrary", "arbitrary"),
            vmem_limit_bytes=VMEM_LIMIT_BYTES),
        name="mixer",
    )(x, gain, w_in, conv_w, pool_w2d, pool_scale, w_out, *side)
    return outs[0], list(outs[1:])


def kernel(x, norm_ffn1, ffn1_w_gate, ffn1_w_up, ffn1_w_down, norm_mix, w_in, conv_w,
           pool_w, pool_scale, w_out, norm_ffn2, ffn2_w_gate, ffn2_w_up, ffn2_w_down,
           norm_final):
    b, s, d_model = x.shape
    depth = norm_ffn1.shape[0]
    assert depth >= 1
    final_gain = norm_final.reshape(1, d_model)
    gc = pool_w.shape[-1]
    ffn1_w = [ffn1_w_gate[0], ffn1_w_up[0], ffn1_w_down[0]]
    for l in range(depth):
        last = l == depth - 1
        mix_w = [w_in[l], pool_w[l].reshape(-1, gc), w_out[l]]
        ffn2_w = [ffn2_w_gate[l], ffn2_w_up[l], ffn2_w_down[l]]
        next_w = [] if last else [ffn1_w_gate[l + 1], ffn1_w_up[l + 1], ffn1_w_down[l + 1]]

        x2d, mix_w = _ffn(x.reshape(b * s, d_model), norm_ffn1[l].reshape(1, d_model),
                          *ffn1_w, final_gain, mix_w, final_norm=False)
        x, ffn2_w = _mixer(x2d.reshape(b, s, d_model), norm_mix[l].reshape(1, d_model),
                           mix_w[0], conv_w[l], mix_w[1], pool_scale[l].reshape(1, -1),
                           mix_w[2], ffn2_w)
        x2d, ffn1_w = _ffn(x.reshape(b * s, d_model), norm_ffn2[l].reshape(1, d_model),
                           *ffn2_w, final_gain, next_w, final_norm=last)
        x = x2d.reshape(b, s, d_model)
    return x
```

```python
import functools

import jax
import jax.numpy as jnp
from jax import lax
from jax.experimental import pallas as pl
from jax.experimental.pallas import tpu as pltpu

EPS = 1e-6
CONV_WIDTH = 3
POOL_WINDOWS = (2, 4, 8, 16)

V7X_SUBLANES = 8
V7X_BF16_ROWS = 16

FFN_TOKEN_TILE = 512
MIXER_TOKEN_TILE = 1024
MIXER_SUB_TILES = 2
CONV_HALO = V7X_SUBLANES
POOL_HALO = 2 * V7X_SUBLANES
LOAD_CHUNKS = 16
LOAD_DEPTH = 4
VMEM_LIMIT_BYTES = 56 * 1024 * 1024


def _rms_norm(x, gain):
    y = x * lax.rsqrt(jnp.mean(x * x, axis=-1, keepdims=True) + EPS)
    return y * gain


def _resident(shape):
    return pl.BlockSpec(shape, lambda *_: (0,) * len(shape), pipeline_mode=pl.Buffered(1))


def _side_specs(weights, grid):
    n_steps = 1
    for g in grid:
        n_steps *= g
    specs, shapes = [], []
    for w in weights:
        rows, cols = w.shape
        hold = 1
        while (rows * hold) % n_steps or (rows * hold // n_steps) % V7X_BF16_ROWS:
            hold *= 2
            assert hold <= n_steps, (rows, n_steps)

        def index_map(*idx, hold=hold):
            step = idx[0]
            for g, i in zip(grid[1:], idx[1:]):
                step = step * g + i
            return (step // hold, 0)

        specs.append(pl.BlockSpec((rows * hold // n_steps, cols), index_map))
        shapes.append(jax.ShapeDtypeStruct(w.shape, jnp.bfloat16))
    return specs, shapes


def _side_cast(src_refs, dst_refs):
    for src, dst in zip(src_refs, dst_refs):
        dst[...] = src[...].astype(jnp.bfloat16)


def _swiglu_residual(x, gain, wg_ref, wu_ref, wd_ref, final_gain, final_norm):
    h = _rms_norm(x, gain).astype(jnp.bfloat16)
    gate = jnp.dot(h, wg_ref[...], preferred_element_type=jnp.float32)
    up = jnp.dot(h, wu_ref[...], preferred_element_type=jnp.float32)
    act = (gate * jax.nn.sigmoid(gate) * up).astype(jnp.bfloat16)
    y = jnp.dot(act, wd_ref[...], preferred_element_type=jnp.float32)
    out = x + 0.5 * y
    if final_norm:
        out = _rms_norm(out, final_gain)
    return out


def _load_cast(jobs):
    chunks = []
    ring_pos = {}
    for w_hbm, w_vmem, stage, sem in jobs:
        rows = stage.shape[1]
        for c in range(w_hbm.shape[0] // rows):
            slot = ring_pos.get(id(stage), 0)
            ring_pos[id(stage)] = (slot + 1) % LOAD_DEPTH
            chunks.append((w_hbm, w_vmem, stage, sem, c * rows, rows, slot))

    def copy(k):
        w_hbm, _, stage, sem, row0, rows, slot = chunks[k]
        return pltpu.make_async_copy(w_hbm.at[pl.ds(row0, rows)], stage.at[slot], sem.at[slot])

    for k in range(min(LOAD_DEPTH, len(chunks))):
        copy(k).start(priority=k % 2)
    for k, (_, w_vmem, stage, _, row0, rows, slot) in enumerate(chunks):
        copy(k).wait()
        w_vmem[pl.ds(row0, rows), :] = stage[slot].astype(jnp.bfloat16)
        if k + LOAD_DEPTH < len(chunks):
            copy(k + LOAD_DEPTH).start(priority=(k + LOAD_DEPTH) % 2)


def _ffn_kernel(*refs, n_side, final_norm, cast_own):
    x_ref, g_ref, wg_ref, wu_ref, wd_ref, gf_ref = refs[:6]
    side_in = refs[6:6 + n_side]
    o_ref = refs[6 + n_side]
    side_out = refs[7 + n_side:7 + 2 * n_side]
    scratch = refs[7 + 2 * n_side:]

    if cast_own:
        wg_bf, wu_bf, wd_bf, stage_in, stage_out, sem_in, sem_out = scratch

        @pl.when(pl.program_id(0) == 0)
        def _():
            _load_cast([(wg_ref, wg_bf, stage_in, sem_in),
                        (wu_ref, wu_bf, stage_in, sem_in),
                        (wd_ref, wd_bf, stage_out, sem_out)])

        wg_ref, wu_ref, wd_ref = wg_bf, wu_bf, wd_bf

    _side_cast(side_in, side_out)
    o_ref[...] = _swiglu_residual(x_ref[...], g_ref[...], wg_ref, wu_ref, wd_ref,
                                  gf_ref[...], final_norm)


def _ffn(x2d, gain, w_gate, w_up, w_down, final_gain, side, *, final_norm):
    n_tok, d_model = x2d.shape
    d_ff = w_gate.shape[1]
    assert n_tok % FFN_TOKEN_TILE == 0
    grid = (n_tok // FFN_TOKEN_TILE,)
    cast_own = w_gate.dtype == jnp.float32
    assert w_up.dtype == w_gate.dtype and w_down.dtype == w_gate.dtype
    tile = pl.BlockSpec((FFN_TOKEN_TILE, d_model), lambda i: (i, 0))
    side_specs, side_shapes = _side_specs(side, grid)
    if cast_own:
        assert d_model % (LOAD_CHUNKS * V7X_BF16_ROWS) == 0
        assert d_ff % (LOAD_CHUNKS * V7X_BF16_ROWS) == 0
        hbm = pl.BlockSpec(memory_space=pl.ANY)
        w_specs = [hbm, hbm, hbm]
        scratch = [
            pltpu.VMEM((d_model, d_ff), jnp.bfloat16),
            pltpu.VMEM((d_model, d_ff), jnp.bfloat16),
            pltpu.VMEM((d_ff, d_model), jnp.bfloat16),
            pltpu.VMEM((LOAD_DEPTH, d_model // LOAD_CHUNKS, d_ff), jnp.float32),
            pltpu.VMEM((LOAD_DEPTH, d_ff // LOAD_CHUNKS, d_model), jnp.float32),
            pltpu.SemaphoreType.DMA((LOAD_DEPTH,)),
            pltpu.SemaphoreType.DMA((LOAD_DEPTH,)),
        ]
    else:
        w_specs = [_resident(w_gate.shape), _resident(w_up.shape), _resident(w_down.shape)]
        scratch = []
    outs = pl.pallas_call(
        functools.partial(_ffn_kernel, n_side=len(side), final_norm=final_norm,
                          cast_own=cast_own),
        grid=grid,
        in_specs=[tile, _resident((1, d_model))] + w_specs + [_resident((1, d_model))]
        + side_specs,
        out_specs=[tile] + side_specs,
        out_shape=[jax.ShapeDtypeStruct(x2d.shape, x2d.dtype)] + side_shapes,
        scratch_shapes=scratch,
        compiler_params=pltpu.CompilerParams(
            dimension_semantics=("arbitrary",), vmem_limit_bytes=VMEM_LIMIT_BYTES),
        name="ffn_final" if final_norm else "ffn",
    )(x2d, gain, w_gate, w_up, w_down, final_gain, *side)
    return outs[0], list(outs[1:])


def _mixer_kernel(*refs, n_side):
    x_ref, g_ref, win_ref, cw_ref, pw_ref, ps_ref, wout_ref = refs[:7]
    side_in = refs[7:7 + n_side]
    o_ref = refs[7 + n_side]
    side_out = refs[8 + n_side:8 + 2 * n_side]
    zbuf, ubuf, wcat = refs[8 + 2 * n_side:]
    j = pl.program_id(1)
    t_tile = x_ref.shape[1]
    n_sub = zbuf.shape[0]
    t_sub = t_tile // n_sub
    d_conv = zbuf.shape[2]
    d_pool = ubuf.shape[2]
    n_groups = len(POOL_WINDOWS)
    gc = d_pool // n_groups

    @pl.when((pl.program_id(0) == 0) & (j == 0))
    def _():
        wcat[0:d_conv, :] = wout_ref[0:d_conv, :]
        for g in range(n_groups):
            lo = g * gc
            p_g = (pw_ref[lo:lo + gc, :] * ps_ref[:, lo:lo + gc]).astype(jnp.bfloat16)
            wcat[d_conv + lo:d_conv + lo + gc, :] = jnp.dot(
                p_g, wout_ref[d_conv + lo:d_conv + lo + gc, :],
                preferred_element_type=jnp.float32).astype(jnp.bfloat16)

    @pl.when(j == 0)
    def _():
        zbuf[0, 0:CONV_HALO, :] = jnp.zeros((CONV_HALO, d_conv), jnp.float32)
        ubuf[0, 0:POOL_HALO, :] = jnp.zeros((POOL_HALO, d_pool), jnp.float32)

    _side_cast(side_in, side_out)

    for k in range(n_sub):
        rows = pl.ds(k * t_sub, t_sub)
        nxt = (k + 1) % n_sub
        x = x_ref[0, rows, :]
        h = _rms_norm(x, g_ref[...]).astype(jnp.bfloat16)

        ubuf[k, POOL_HALO:, :] = jnp.dot(h, win_ref[:, 3 * d_conv:],
                                         preferred_element_type=jnp.float32)
        head = POOL_HALO
        head_pos = (j * t_tile + k * t_sub + 1
                    + lax.broadcasted_iota(jnp.int32, (head, gc), 0))
        y_b = []
        for g, w in enumerate(POOL_WINDOWS):
            ue = ubuf[k, :, g * gc:(g + 1) * gc]
            s = ue
            span = 1
            while span < w:
                s = s + pltpu.roll(s, span, axis=0)
                span *= 2
            s = s[POOL_HALO:, :]
            head_count = jnp.minimum(head_pos, w).astype(jnp.float32)
            mean = jnp.concatenate(
                [s[:head, :] / head_count, s[head:, :] * (1.0 / w)], axis=0)
            y_b.append(mean - ue[POOL_HALO:, :])
        ubuf[nxt, 0:POOL_HALO, :] = ubuf[k, t_sub:, :]
        y = jnp.dot(jnp.concatenate(y_b, axis=-1).astype(jnp.bfloat16), wcat[d_conv:, :],
                    preferred_element_type=jnp.float32)

        proj = jnp.dot(h, win_ref[:, 0:3 * d_conv], preferred_element_type=jnp.float32)
        v = proj[:, 0:d_conv]
        gate_b = proj[:, d_conv:2 * d_conv]
        gate_c = proj[:, 2 * d_conv:3 * d_conv]
        zbuf[k, CONV_HALO:, :] = gate_c * v
        ze = zbuf[k]
        cw = cw_ref[...]
        conv = cw[CONV_WIDTH - 1:CONV_WIDTH, :] * ze
        for tap in range(1, CONV_WIDTH):
            conv = conv + (cw[CONV_WIDTH - 1 - tap:CONV_WIDTH - tap, :]
                           * pltpu.roll(ze, tap, axis=0))
        y_a = gate_b * conv[CONV_HALO:, :]
        zbuf[nxt, 0:CONV_HALO, :] = ze[t_sub:, :]
        y = y + jnp.dot(y_a.astype(jnp.bfloat16), wcat[0:d_conv, :],
                        preferred_element_type=jnp.float32)
        o_ref[0, rows, :] = x + y


def _mixer(x, gain, w_in, conv_w, pool_w2d, pool_scale, w_out, side):
    b, s, d_model = x.shape
    d_conv = conv_w.shape[1]
    d_pool = pool_scale.shape[1]
    assert s % MIXER_TOKEN_TILE == 0 and MIXER_TOKEN_TILE % MIXER_SUB_TILES == 0
    t_sub = MIXER_TOKEN_TILE // MIXER_SUB_TILES
    assert t_sub % V7X_SUBLANES == 0 and t_sub >= POOL_HALO
    grid = (b, s // MIXER_TOKEN_TILE)
    tile = pl.BlockSpec((1, MIXER_TOKEN_TILE, d_model), lambda i, j: (i, j, 0))
    side_specs, side_shapes = _side_specs(side, grid)
    outs = pl.pallas_call(
        functools.partial(_mixer_kernel, n_side=len(side)),
        grid=grid,
        in_specs=[
            tile,
            _resident((1, d_model)),
            _resident(w_in.shape),
            _resident(conv_w.shape),
            _resident(pool_w2d.shape),
            _resident(pool_scale.shape),
            _resident(w_out.shape),
        ] + side_specs,
        out_specs=[tile] + side_specs,
        out_shape=[jax.ShapeDtypeStruct(x.shape, x.dtype)] + side_shapes,
        scratch_shapes=[
            pltpu.VMEM((MIXER_SUB_TILES, CONV_HALO + t_sub, d_conv), jnp.float32),
            pltpu.VMEM((MIXER_SUB_TILES, POOL_HALO + t_sub, d_pool), jnp.float32),
            pltpu.VMEM(w_out.shape, jnp.bfloat16),
        ],
        compiler_params=pltpu.CompilerParams(
            dimension_semantics=("arbitrary", "arbitrary"),
            vmem_limit_bytes=VMEM_LIMIT_BYTES),
        name="mixer",
    )(x, gain, w_in, conv_w, pool_w2d, pool_scale, w_out, *side)
    return outs[0], list(outs[1:])


def kernel(x, norm_ffn1, ffn1_w_gate, ffn1_w_up, ffn1_w_down, norm_mix, w_in, conv_w,
           pool_w, pool_scale, w_out, norm_ffn2, ffn2_w_gate, ffn2_w_up, ffn2_w_down,
           norm_final):
    b, s, d_model = x.shape
    depth = norm_ffn1.shape[0]
    assert depth >= 1
    final_gain = norm_final.reshape(1, d_model)
    gc = pool_w.shape[-1]
    ffn1_w = [ffn1_w_gate[0], ffn1_w_up[0], ffn1_w_down[0]]
    for l in range(depth):
        last = l == depth - 1
        mix_w = [w_in[l], w_out[l]]
        ffn2_w = [ffn2_w_gate[l], ffn2_w_up[l], ffn2_w_down[l]]
        next_w = [] if last else [ffn1_w_gate[l + 1], ffn1_w_up[l + 1], ffn1_w_down[l + 1]]

        x2d, mix_w = _ffn(x.reshape(b * s, d_model), norm_ffn1[l].reshape(1, d_model),
                          *ffn1_w, final_gain, mix_w, final_norm=False)
        x, ffn2_w = _mixer(x2d.reshape(b, s, d_model), norm_mix[l].reshape(1, d_model),
                           mix_w[0], conv_w[l], pool_w[l].reshape(-1, gc),
                           pool_scale[l].reshape(1, -1), mix_w[1], ffn2_w)
        x2d, ffn1_w = _ffn(x.reshape(b * s, d_model), norm_ffn2[l].reshape(1, d_model),
                           *ffn2_w, final_gain, next_w, final_norm=last)
        x = x2d.reshape(b, s, d_model)
    return x
```

```python
import functools

import jax
import jax.numpy as jnp
from jax import lax
from jax.experimental import pallas as pl
from jax.experimental.pallas import tpu as pltpu

EPS = 1e-6
CONV_WIDTH = 3
POOL_WINDOWS = (2, 4, 8, 16)

V7X_SUBLANES = 8
V7X_BF16_ROWS = 16

FFN_TOKEN_TILE = 1024
FFN_SUB_TILES = 4
MIXER_TOKEN_TILE = 1024
MIXER_SUB_TILES = 2
CONV_HALO = V7X_SUBLANES
POOL_HALO = 2 * V7X_SUBLANES
LOAD_CHUNKS = 16
LOAD_DEPTH = 7
VMEM_LIMIT_BYTES = 56 * 1024 * 1024


def _rms_norm(x, gain):
    y = x * lax.rsqrt(jnp.mean(x * x, axis=-1, keepdims=True) + EPS)
    return y * gain


def _resident(shape):
    return pl.BlockSpec(shape, lambda *_: (0,) * len(shape), pipeline_mode=pl.Buffered(1))


def _side_specs(weights, grid):
    n_steps = 1
    for g in grid:
        n_steps *= g
    specs, shapes = [], []
    for w in weights:
        rows, cols = w.shape
        hold = 1
        while (rows * hold) % n_steps or (rows * hold // n_steps) % V7X_BF16_ROWS:
            hold *= 2
            assert hold <= n_steps, (rows, n_steps)

        def index_map(*idx, hold=hold):
            step = idx[0]
            for g, i in zip(grid[1:], idx[1:]):
                step = step * g + i
            return (step // hold, 0)

        specs.append(pl.BlockSpec((rows * hold // n_steps, cols), index_map))
        shapes.append(jax.ShapeDtypeStruct(w.shape, jnp.bfloat16))
    return specs, shapes


def _side_cast(src_refs, dst_refs):
    for src, dst in zip(src_refs, dst_refs):
        dst[...] = src[...].astype(jnp.bfloat16)


def _swiglu_residual(x, gain, wg_ref, wu_ref, wd_ref, final_gain, final_norm):
    h = _rms_norm(x, gain).astype(jnp.bfloat16)
    gate = jnp.dot(h, wg_ref[...], preferred_element_type=jnp.float32)
    up = jnp.dot(h, wu_ref[...], preferred_element_type=jnp.float32)
    act = (gate * jax.nn.sigmoid(gate) * up).astype(jnp.bfloat16)
    y = jnp.dot(act, wd_ref[...], preferred_element_type=jnp.float32)
    out = x + 0.5 * y
    if final_norm:
        out = _rms_norm(out, final_gain)
    return out


def _load_cast(jobs):
    chunks = []
    ring_pos = {}
    for w_hbm, w_vmem, stage, sem in jobs:
        rows = stage.shape[1]
        for c in range(w_hbm.shape[0] // rows):
            slot = ring_pos.get(id(stage), 0)
            ring_pos[id(stage)] = (slot + 1) % LOAD_DEPTH
            chunks.append((w_hbm, w_vmem, stage, sem, c * rows, rows, slot))

    def copy(k):
        w_hbm, _, stage, sem, row0, rows, slot = chunks[k]
        return pltpu.make_async_copy(w_hbm.at[pl.ds(row0, rows)], stage.at[slot], sem.at[slot])

    for k in range(min(LOAD_DEPTH, len(chunks))):
        copy(k).start()
    for k, (_, w_vmem, stage, _, row0, rows, slot) in enumerate(chunks):
        copy(k).wait()
        w_vmem[pl.ds(row0, rows), :] = stage[slot].astype(jnp.bfloat16)
        if k + LOAD_DEPTH < len(chunks):
            copy(k + LOAD_DEPTH).start()


def _ffn_kernel(*refs, n_side, final_norm, cast_own):
    x_ref, g_ref, wg_ref, wu_ref, wd_ref, gf_ref = refs[:6]
    side_in = refs[6:6 + n_side]
    o_ref = refs[6 + n_side]
    side_out = refs[7 + n_side:7 + 2 * n_side]
    scratch = refs[7 + 2 * n_side:]

    if cast_own:
        wg_bf, wu_bf, wd_bf, stage_in, stage_out, sem_in, sem_out = scratch

        @pl.when(pl.program_id(0) == 0)
        def _():
            _load_cast([(wg_ref, wg_bf, stage_in, sem_in),
                        (wu_ref, wu_bf, stage_in, sem_in),
                        (wd_ref, wd_bf, stage_out, sem_out)])

        wg_ref, wu_ref, wd_ref = wg_bf, wu_bf, wd_bf

    _side_cast(side_in, side_out)
    t_sub = x_ref.shape[0] // FFN_SUB_TILES
    for k in range(FFN_SUB_TILES):
        rows = pl.ds(k * t_sub, t_sub)
        o_ref[rows, :] = _swiglu_residual(x_ref[rows, :], g_ref[...], wg_ref, wu_ref, wd_ref,
                                          gf_ref[...], final_norm)


def _ffn(x2d, gain, w_gate, w_up, w_down, final_gain, side, *, final_norm):
    n_tok, d_model = x2d.shape
    d_ff = w_gate.shape[1]
    assert n_tok % FFN_TOKEN_TILE == 0
    grid = (n_tok // FFN_TOKEN_TILE,)
    cast_own = w_gate.dtype == jnp.float32
    assert w_up.dtype == w_gate.dtype and w_down.dtype == w_gate.dtype
    tile = pl.BlockSpec((FFN_TOKEN_TILE, d_model), lambda i: (i, 0))
    side_specs, side_shapes = _side_specs(side, grid)
    if cast_own:
        assert d_model % (LOAD_CHUNKS * V7X_BF16_ROWS) == 0
        assert d_ff % (LOAD_CHUNKS * V7X_BF16_ROWS) == 0
        hbm = pl.BlockSpec(memory_space=pl.ANY)
        w_specs = [hbm, hbm, hbm]
        scratch = [
            pltpu.VMEM((d_model, d_ff), jnp.bfloat16),
            pltpu.VMEM((d_model, d_ff), jnp.bfloat16),
            pltpu.VMEM((d_ff, d_model), jnp.bfloat16),
            pltpu.VMEM((LOAD_DEPTH, d_model // LOAD_CHUNKS, d_ff), jnp.float32),
            pltpu.VMEM((LOAD_DEPTH, d_ff // LOAD_CHUNKS, d_model), jnp.float32),
            pltpu.SemaphoreType.DMA((LOAD_DEPTH,)),
            pltpu.SemaphoreType.DMA((LOAD_DEPTH,)),
        ]
    else:
        w_specs = [_resident(w_gate.shape), _resident(w_up.shape), _resident(w_down.shape)]
        scratch = []
    outs = pl.pallas_call(
        functools.partial(_ffn_kernel, n_side=len(side), final_norm=final_norm,
                          cast_own=cast_own),
        grid=grid,
        in_specs=[tile, _resident((1, d_model))] + w_specs + [_resident((1, d_model))]
        + side_specs,
        out_specs=[tile] + side_specs,
        out_shape=[jax.ShapeDtypeStruct(x2d.shape, x2d.dtype)] + side_shapes,
        scratch_shapes=scratch,
        compiler_params=pltpu.CompilerParams(
            dimension_semantics=("arbitrary",), vmem_limit_bytes=VMEM_LIMIT_BYTES),
        name="ffn_final" if final_norm else "ffn",
    )(x2d, gain, w_gate, w_up, w_down, final_gain, *side)
    return outs[0], list(outs[1:])


def _mixer_kernel(*refs, n_side):
    x_ref, g_ref, win_ref, cw_ref, pw_ref, ps_ref, wout_ref = refs[:7]
    side_in = refs[7:7 + n_side]
    o_ref = refs[7 + n_side]
    side_out = refs[8 + n_side:8 + 2 * n_side]
    zbuf, ubuf, wcat = refs[8 + 2 * n_side:]
    j = pl.program_id(1)
    t_tile = x_ref.shape[1]
    n_sub = zbuf.shape[0]
    t_sub = t_tile // n_sub
    d_conv = zbuf.shape[2]
    d_pool = ubuf.shape[2]
    n_groups = len(POOL_WINDOWS)
    gc = d_pool // n_groups

    @pl.when((pl.program_id(0) == 0) & (j == 0))
    def _():
        wcat[0:d_conv, :] = wout_ref[0:d_conv, :]
        for g in range(n_groups):
            lo = g * gc
            p_g = (pw_ref[lo:lo + gc, :] * ps_ref[:, lo:lo + gc]).astype(jnp.bfloat16)
            wcat[d_conv + lo:d_conv + lo + gc, :] = jnp.dot(
                p_g, wout_ref[d_conv + lo:d_conv + lo + gc, :],
                preferred_element_type=jnp.float32).astype(jnp.bfloat16)

    @pl.when(j == 0)
    def _():
        zbuf[0, 0:CONV_HALO, :] = jnp.zeros((CONV_HALO, d_conv), jnp.float32)
        ubuf[0, 0:POOL_HALO, :] = jnp.zeros((POOL_HALO, d_pool), jnp.float32)

    _side_cast(side_in, side_out)

    for k in range(n_sub):
        rows = pl.ds(k * t_sub, t_sub)
        nxt = (k + 1) % n_sub
        x = x_ref[0, rows, :]
        h = _rms_norm(x, g_ref[...]).astype(jnp.bfloat16)

        ubuf[k, POOL_HALO:, :] = jnp.dot(h, win_ref[:, 3 * d_conv:],
                                         preferred_element_type=jnp.float32)
        head = POOL_HALO
        head_pos = (j * t_tile + k * t_sub + 1
                    + lax.broadcasted_iota(jnp.int32, (head, gc), 0))
        y_b = []
        for g, w in enumerate(POOL_WINDOWS):
            ue = ubuf[k, :, g * gc:(g + 1) * gc]
            s = ue
            span = 1
            while span < w:
                s = s + pltpu.roll(s, span, axis=0)
                span *= 2
            s = s[POOL_HALO:, :]
            head_count = jnp.minimum(head_pos, w).astype(jnp.float32)
            mean = jnp.concatenate(
                [s[:head, :] / head_count, s[head:, :] * (1.0 / w)], axis=0)
            y_b.append(mean - ue[POOL_HALO:, :])
        ubuf[nxt, 0:POOL_HALO, :] = ubuf[k, t_sub:, :]
        y = jnp.dot(jnp.concatenate(y_b, axis=-1).astype(jnp.bfloat16), wcat[d_conv:, :],
                    preferred_element_type=jnp.float32)

        proj = jnp.dot(h, win_ref[:, 0:3 * d_conv], preferred_element_type=jnp.float32)
        v = proj[:, 0:d_conv]
        gate_b = proj[:, d_conv:2 * d_conv]
        gate_c = proj[:, 2 * d_conv:3 * d_conv]
        zbuf[k, CONV_HALO:, :] = gate_c * v
        ze = zbuf[k]
        cw = cw_ref[...]
        conv = cw[CONV_WIDTH - 1:CONV_WIDTH, :] * ze
        for tap in range(1, CONV_WIDTH):
            conv = conv + (cw[CONV_WIDTH - 1 - tap:CONV_WIDTH - tap, :]
                           * pltpu.roll(ze, tap, axis=0))
        y_a = gate_b * conv[CONV_HALO:, :]
        zbuf[nxt, 0:CONV_HALO, :] = ze[t_sub:, :]
        y = y + jnp.dot(y_a.astype(jnp.bfloat16), wcat[0:d_conv, :],
                        preferred_element_type=jnp.float32)
        o_ref[0, rows, :] = x + y


def _mixer(x, gain, w_in, conv_w, pool_w2d, pool_scale, w_out, side):
    b, s, d_model = x.shape
    d_conv = conv_w.shape[1]
    d_pool = pool_scale.shape[1]
    assert s % MIXER_TOKEN_TILE == 0 and MIXER_TOKEN_TILE % MIXER_SUB_TILES == 0
    t_sub = MIXER_TOKEN_TILE // MIXER_SUB_TILES
    assert t_sub % V7X_SUBLANES == 0 and t_sub >= POOL_HALO
    grid = (b, s // MIXER_TOKEN_TILE)
    tile = pl.BlockSpec((1, MIXER_TOKEN_TILE, d_model), lambda i, j: (i, j, 0))
    side_specs, side_shapes = _side_specs(side, grid)
    outs = pl.pallas_call(
        functools.partial(_mixer_kernel, n_side=len(side)),
        grid=grid,
        in_specs=[
            tile,
            _resident((1, d_model)),
            _resident(w_in.shape),
            _resident(conv_w.shape),
            _resident(pool_w2d.shape),
            _resident(pool_scale.shape),
            _resident(w_out.shape),
        ] + side_specs,
        out_specs=[tile] + side_specs,
        out_shape=[jax.ShapeDtypeStruct(x.shape, x.dtype)] + side_shapes,
        scratch_shapes=[
            pltpu.VMEM((MIXER_SUB_TILES, CONV_HALO + t_sub, d_conv), jnp.float32),
            pltpu.VMEM((MIXER_SUB_TILES, POOL_HALO + t_sub, d_pool), jnp.float32),
            pltpu.VMEM(w_out.shape, jnp.bfloat16),
        ],
        compiler_params=pltpu.CompilerParams(
            dimension_semantics=("arbitrary", "arbitrary"),
            vmem_limit_bytes=VMEM_LIMIT_BYTES),
        name="mixer",
    )(x, gain, w_in, conv_w, pool_w2d, pool_scale, w_out, *side)
    return outs[0], list(outs[1:])


def kernel(x, norm_ffn1, ffn1_w_gate, ffn1_w_up, ffn1_w_down, norm_mix, w_in, conv_w,
           pool_w, pool_scale, w_out, norm_ffn2, ffn2_w_gate, ffn2_w_up, ffn2_w_down,
           norm_final):
    b, s, d_model = x.shape
    depth = norm_ffn1.shape[0]
    assert depth >= 1
    final_gain = norm_final.reshape(1, d_model)
    gc = pool_w.shape[-1]
    ffn1_w = [ffn1_w_gate[0], ffn1_w_up[0], ffn1_w_down[0]]
    for l in range(depth):
        last = l == depth - 1
        mix_w = [w_in[l], w_out[l]]
        ffn2_w = [ffn2_w_gate[l], ffn2_w_up[l], ffn2_w_down[l]]
        next_w = [] if last else [ffn1_w_gate[l + 1], ffn1_w_up[l + 1], ffn1_w_down[l + 1]]

        x2d, mix_w = _ffn(x.reshape(b * s, d_model), norm_ffn1[l].reshape(1, d_model),
                          *ffn1_w, final_gain, mix_w, final_norm=False)
        x, ffn2_w = _mixer(x2d.reshape(b, s, d_model), norm_mix[l].reshape(1, d_model),
                           mix_w[0], conv_w[l], pool_w[l].reshape(-1, gc),
                           pool_scale[l].reshape(1, -1), mix_w[1], ffn2_w)
        x2d, ffn1_w = _ffn(x.reshape(b * s, d_model), norm_ffn2[l].reshape(1, d_model),
                           *ffn2_w, final_gain, next_w, final_norm=last)
        x = x2d.reshape(b, s, d_model)
    return x
```

```python
import functools

import jax
import jax.numpy as jnp
from jax import lax
from jax.experimental import pallas as pl
from jax.experimental.pallas import tpu as pltpu

EPS = 1e-6
CONV_WIDTH = 3
POOL_WINDOWS = (2, 4, 8, 16)

V7X_SUBLANES = 8
V7X_BF16_ROWS = 16

FFN_TOKEN_TILE = 1024
FFN_SUB_TILES = 4
MIXER_TOKEN_TILE = 1024
MIXER_SUB_TILES = 2
CONV_HALO = V7X_SUBLANES
POOL_HALO = 2 * V7X_SUBLANES
FF_CHUNK = 256
STAGE_DEPTH = 3
VMEM_LIMIT_BYTES = 56 * 1024 * 1024


def _rms_norm(x, gain):
    y = x * lax.rsqrt(jnp.mean(x * x, axis=-1, keepdims=True) + EPS)
    return y * gain


def _resident(shape):
    return pl.BlockSpec(shape, lambda *_: (0,) * len(shape), pipeline_mode=pl.Buffered(1))


def _side_specs(weights, grid):
    n_steps = 1
    for g in grid:
        n_steps *= g
    specs, shapes = [], []
    for w in weights:
        rows, cols = w.shape
        hold = 1
        while (rows * hold) % n_steps or (rows * hold // n_steps) % V7X_BF16_ROWS:
            hold *= 2
            assert hold <= n_steps, (rows, n_steps)

        def index_map(*idx, hold=hold):
            step = idx[0]
            for g, i in zip(grid[1:], idx[1:]):
                step = step * g + i
            return (step // hold, 0)

        specs.append(pl.BlockSpec((rows * hold // n_steps, cols), index_map))
        shapes.append(jax.ShapeDtypeStruct(w.shape, jnp.bfloat16))
    return specs, shapes


def _side_cast(src_refs, dst_refs):
    for src, dst in zip(src_refs, dst_refs):
        dst[...] = src[...].astype(jnp.bfloat16)


def _swiglu_residual(x, gain, wg_ref, wu_ref, wd_ref, final_gain, final_norm):
    h = _rms_norm(x, gain).astype(jnp.bfloat16)
    gate = jnp.dot(h, wg_ref[...], preferred_element_type=jnp.float32)
    up = jnp.dot(h, wu_ref[...], preferred_element_type=jnp.float32)
    act = (gate * jax.nn.sigmoid(gate) * up).astype(jnp.bfloat16)
    y = jnp.dot(act, wd_ref[...], preferred_element_type=jnp.float32)
    out = x + 0.5 * y
    if final_norm:
        out = _rms_norm(out, final_gain)
    return out


def _first_step(x, gain, w_hbm, w_bf, stages, sems, h_buf, y_buf, final_gain, final_norm):
    n_chunks = w_bf[0].shape[1] // FF_CHUNK
    depth = n_chunks if stages is None else stages[0].shape[0]

    def piece(w, c):
        span = pl.ds(c * FF_CHUNK, FF_CHUNK)
        return (span, slice(None)) if w == 2 else (slice(None), span)

    def copy(w, c):
        dst = w_bf[w].at[piece(w, c)] if stages is None else stages[w].at[c % depth]
        return pltpu.make_async_copy(w_hbm[w].at[piece(w, c)], dst, sems.at[w, c % depth])

    for c in range(min(depth, n_chunks)):
        for w in range(3):
            copy(w, c).start()
    h_buf[...] = _rms_norm(x, gain).astype(jnp.bfloat16)
    for c in range(n_chunks):
        for w in range(3):
            copy(w, c).wait()
            if stages is not None:
                w_bf[w][piece(w, c)] = stages[w][c % depth].astype(jnp.bfloat16)
        if c + depth < n_chunks:
            for w in range(3):
                copy(w, c + depth).start()
        span = pl.ds(c * FF_CHUNK, FF_CHUNK)
        h = h_buf[...]
        gate = jnp.dot(h, w_bf[0][:, span], preferred_element_type=jnp.float32)
        up = jnp.dot(h, w_bf[1][:, span], preferred_element_type=jnp.float32)
        act = (gate * jax.nn.sigmoid(gate) * up).astype(jnp.bfloat16)
        part = jnp.dot(act, w_bf[2][span, :], preferred_element_type=jnp.float32)
        if c == 0:
            y_buf[...] = part
        else:
            y_buf[...] += part
    out = x + 0.5 * y_buf[...]
    if final_norm:
        out = _rms_norm(out, final_gain)
    return out


def _ffn_kernel(*refs, n_side, final_norm, cast_own):
    x_ref, g_ref, wg_hbm, wu_hbm, wd_hbm, gf_ref = refs[:6]
    side_in = refs[6:6 + n_side]
    o_ref = refs[6 + n_side]
    side_out = refs[7 + n_side:7 + 2 * n_side]
    wg_bf, wu_bf, wd_bf, h_buf, y_buf = refs[7 + 2 * n_side:12 + 2 * n_side]
    sems = refs[-1]
    stages = refs[12 + 2 * n_side:-1] if cast_own else None
    first = pl.program_id(0) == 0

    _side_cast(side_in, side_out)

    @pl.when(first)
    def _():
        o_ref[...] = _first_step(x_ref[...], g_ref[...], (wg_hbm, wu_hbm, wd_hbm),
                                 (wg_bf, wu_bf, wd_bf), stages, sems, h_buf, y_buf,
                                 gf_ref[...], final_norm)

    @pl.when(jnp.logical_not(first))
    def _():
        t_sub = x_ref.shape[0] // FFN_SUB_TILES
        for k in range(FFN_SUB_TILES):
            rows = pl.ds(k * t_sub, t_sub)
            o_ref[rows, :] = _swiglu_residual(x_ref[rows, :], g_ref[...], wg_bf, wu_bf,
                                              wd_bf, gf_ref[...], final_norm)


def _ffn(x2d, gain, w_gate, w_up, w_down, final_gain, side, *, final_norm):
    n_tok, d_model = x2d.shape
    d_ff = w_gate.shape[1]
    assert n_tok % FFN_TOKEN_TILE == 0 and d_ff % FF_CHUNK == 0
    grid = (n_tok // FFN_TOKEN_TILE,)
    cast_own = w_gate.dtype == jnp.float32
    assert w_up.dtype == w_gate.dtype and w_down.dtype == w_gate.dtype
    tile = pl.BlockSpec((FFN_TOKEN_TILE, d_model), lambda i: (i, 0))
    hbm = pl.BlockSpec(memory_space=pl.ANY)
    side_specs, side_shapes = _side_specs(side, grid)
    scratch = [
        pltpu.VMEM((d_model, d_ff), jnp.bfloat16),
        pltpu.VMEM((d_model, d_ff), jnp.bfloat16),
        pltpu.VMEM((d_ff, d_model), jnp.bfloat16),
        pltpu.VMEM((FFN_TOKEN_TILE, d_model), jnp.bfloat16),
        pltpu.VMEM((FFN_TOKEN_TILE, d_model), jnp.float32),
    ]
    if cast_own:
        scratch += [
            pltpu.VMEM((STAGE_DEPTH, d_model, FF_CHUNK), jnp.float32),
            pltpu.VMEM((STAGE_DEPTH, d_model, FF_CHUNK), jnp.float32),
            pltpu.VMEM((STAGE_DEPTH, FF_CHUNK, d_model), jnp.float32),
            pltpu.SemaphoreType.DMA((3, STAGE_DEPTH)),
        ]
    else:
        scratch += [pltpu.SemaphoreType.DMA((3, d_ff // FF_CHUNK))]
    outs = pl.pallas_call(
        functools.partial(_ffn_kernel, n_side=len(side), final_norm=final_norm,
                          cast_own=cast_own),
        grid=grid,
        in_specs=[tile, _resident((1, d_model)), hbm, hbm, hbm, _resident((1, d_model))]
        + side_specs,
        out_specs=[tile] + side_specs,
        out_shape=[jax.ShapeDtypeStruct(x2d.shape, x2d.dtype)] + side_shapes,
        scratch_shapes=scratch,
        compiler_params=pltpu.CompilerParams(
            dimension_semantics=("arbitrary",), vmem_limit_bytes=VMEM_LIMIT_BYTES),
        name="ffn_final" if final_norm else "ffn",
    )(x2d, gain, w_gate, w_up, w_down, final_gain, *side)
    return outs[0], list(outs[1:])


def _mixer_kernel(*refs, n_side):
    x_ref, g_ref, win_ref, cw_ref, pw_ref, ps_ref, wout_ref = refs[:7]
    side_in = refs[7:7 + n_side]
    o_ref = refs[7 + n_side]
    side_out = refs[8 + n_side:8 + 2 * n_side]
    zbuf, ubuf, wcat = refs[8 + 2 * n_side:]
    j = pl.program_id(1)
    t_tile = x_ref.shape[1]
    n_sub = zbuf.shape[0]
    t_sub = t_tile // n_sub
    d_conv = zbuf.shape[2]
    d_pool = ubuf.shape[2]
    n_groups = len(POOL_WINDOWS)
    gc = d_pool // n_groups

    @pl.when((pl.program_id(0) == 0) & (j == 0))
    def _():
        wcat[0:d_conv, :] = wout_ref[0:d_conv, :]
        for g in range(n_groups):
            lo = g * gc
            p_g = (pw_ref[lo:lo + gc, :] * ps_ref[:, lo:lo + gc]).astype(jnp.bfloat16)
            wcat[d_conv + lo:d_conv + lo + gc, :] = jnp.dot(
                p_g, wout_ref[d_conv + lo:d_conv + lo + gc, :],
                preferred_element_type=jnp.float32).astype(jnp.bfloat16)

    @pl.when(j == 0)
    def _():
        zbuf[0, 0:CONV_HALO, :] = jnp.zeros((CONV_HALO, d_conv), jnp.float32)
        ubuf[0, 0:POOL_HALO, :] = jnp.zeros((POOL_HALO, d_pool), jnp.float32)

    _side_cast(side_in, side_out)

    for k in range(n_sub):
        rows = pl.ds(k * t_sub, t_sub)
        nxt = (k + 1) % n_sub
        x = x_ref[0, rows, :]
        h = _rms_norm(x, g_ref[...]).astype(jnp.bfloat16)

        ubuf[k, POOL_HALO:, :] = jnp.dot(h, win_ref[:, 3 * d_conv:],
                                         preferred_element_type=jnp.float32)
        head = POOL_HALO
        head_pos = (j * t_tile + k * t_sub + 1
                    + lax.broadcasted_iota(jnp.int32, (head, gc), 0))
        y_b = []
        for g, w in enumerate(POOL_WINDOWS):
            ue = ubuf[k, :, g * gc:(g + 1) * gc]
            s = ue
            span = 1
            while span < w:
                s = s + pltpu.roll(s, span, axis=0)
                span *= 2
            s = s[POOL_HALO:, :]
            head_count = jnp.minimum(head_pos, w).astype(jnp.float32)
            mean = jnp.concatenate(
                [s[:head, :] / head_count, s[head:, :] * (1.0 / w)], axis=0)
            y_b.append(mean - ue[POOL_HALO:, :])
        ubuf[nxt, 0:POOL_HALO, :] = ubuf[k, t_sub:, :]
        y = jnp.dot(jnp.concatenate(y_b, axis=-1).astype(jnp.bfloat16), wcat[d_conv:, :],
                    preferred_element_type=jnp.float32)

        proj = jnp.dot(h, win_ref[:, 0:3 * d_conv], preferred_element_type=jnp.float32)
        v = proj[:, 0:d_conv]
        gate_b = proj[:, d_conv:2 * d_conv]
        gate_c = proj[:, 2 * d_conv:3 * d_conv]
        zbuf[k, CONV_HALO:, :] = gate_c * v
        ze = zbuf[k]
        cw = cw_ref[...]
        conv = cw[CONV_WIDTH - 1:CONV_WIDTH, :] * ze
        for tap in range(1, CONV_WIDTH):
            conv = conv + (cw[CONV_WIDTH - 1 - tap:CONV_WIDTH - tap, :]
                           * pltpu.roll(ze, tap, axis=0))
        y_a = gate_b * conv[CONV_HALO:, :]
        zbuf[nxt, 0:CONV_HALO, :] = ze[t_sub:, :]
        y = y + jnp.dot(y_a.astype(jnp.bfloat16), wcat[0:d_conv, :],
                        preferred_element_type=jnp.float32)
        o_ref[0, rows, :] = x + y


def _mixer(x, gain, w_in, conv_w, pool_w2d, pool_scale, w_out, side):
    b, s, d_model = x.shape
    d_conv = conv_w.shape[1]
    d_pool = pool_scale.shape[1]
    assert s % MIXER_TOKEN_TILE == 0 and MIXER_TOKEN_TILE % MIXER_SUB_TILES == 0
    t_sub = MIXER_TOKEN_TILE // MIXER_SUB_TILES
    assert t_sub % V7X_SUBLANES == 0 and t_sub >= POOL_HALO
    grid = (b, s // MIXER_TOKEN_TILE)
    tile = pl.BlockSpec((1, MIXER_TOKEN_TILE, d_model), lambda i, j: (i, j, 0))
    side_specs, side_shapes = _side_specs(side, grid)
    outs = pl.pallas_call(
        functools.partial(_mixer_kernel, n_side=len(side)),
        grid=grid,
        in_specs=[
            tile,
            _resident((1, d_model)),
            _resident(w_in.shape),
            _resident(conv_w.shape),
            _resident(pool_w2d.shape),
            _resident(pool_scale.shape),
            _resident(w_out.shape),
        ] + side_specs,
        out_specs=[tile] + side_specs,
        out_shape=[jax.ShapeDtypeStruct(x.shape, x.dtype)] + side_shapes,
        scratch_shapes=[
            pltpu.VMEM((MIXER_SUB_TILES, CONV_HALO + t_sub, d_conv), jnp.float32),
            pltpu.VMEM((MIXER_SUB_TILES, POOL_HALO + t_sub, d_pool), jnp.float32),
            pltpu.VMEM(w_out.shape, jnp.bfloat16),
        ],
        compiler_params=pltpu.CompilerParams(
            dimension_semantics=("arbitrary", "arbitrary"),
            vmem_limit_bytes=VMEM_LIMIT_BYTES),
        name="mixer",
    )(x, gain, w_in, conv_w, pool_w2d, pool_scale, w_out, *side)
    return outs[0], list(outs[1:])


def kernel(x, norm_ffn1, ffn1_w_gate, ffn1_w_up, ffn1_w_down, norm_mix, w_in, conv_w,
           pool_w, pool_scale, w_out, norm_ffn2, ffn2_w_gate, ffn2_w_up, ffn2_w_down,
           norm_final):
    b, s, d_model = x.shape
    depth = norm_ffn1.shape[0]
    assert depth >= 1
    final_gain = norm_final.reshape(1, d_model)
    gc = pool_w.shape[-1]
    ffn1_w = [ffn1_w_gate[0], ffn1_w_up[0], ffn1_w_down[0]]
    for l in range(depth):
        last = l == depth - 1
        mix_w = [w_in[l], w_out[l]]
        ffn2_w = [ffn2_w_gate[l], ffn2_w_up[l], ffn2_w_down[l]]
        next_w = [] if last else [ffn1_w_gate[l + 1], ffn1_w_up[l + 1], ffn1_w_down[l + 1]]

        x2d, mix_w = _ffn(x.reshape(b * s, d_model), norm_ffn1[l].reshape(1, d_model),
                          *ffn1_w, final_gain, mix_w, final_norm=False)
        x, ffn2_w = _mixer(x2d.reshape(b, s, d_model), norm_mix[l].reshape(1, d_model),
                           mix_w[0], conv_w[l], pool_w[l].reshape(-1, gc),
                           pool_scale[l].reshape(1, -1), mix_w[1], ffn2_w)
        x2d, ffn1_w = _ffn(x.reshape(b * s, d_model), norm_ffn2[l].reshape(1, d_model),
                           *ffn2_w, final_gain, next_w, final_norm=last)
        x = x2d.reshape(b, s, d_model)
    return x
```

```python
import functools

import jax
import jax.numpy as jnp
from jax import lax
from jax.experimental import pallas as pl
from jax.experimental.pallas import tpu as pltpu

EPS = 1e-6
CONV_WIDTH = 3
POOL_WINDOWS = (2, 4, 8, 16)

V7X_SUBLANES = 8
V7X_BF16_ROWS = 16

FFN_TOKEN_TILE = 1024
FFN_SUB_TILES = 4
MIXER_TOKEN_TILE = 1024
MIXER_SUB_TILES = 2
CONV_HALO = V7X_SUBLANES
POOL_HALO = 2 * V7X_SUBLANES
LOAD_CHUNKS = 16
LOAD_DEPTH = 7
VMEM_LIMIT_BYTES = 56 * 1024 * 1024


def _rms_norm(x, gain):
    y = x * lax.rsqrt(jnp.mean(x * x, axis=-1, keepdims=True) + EPS)
    return y * gain


def _resident(shape):
    return pl.BlockSpec(shape, lambda *_: (0,) * len(shape), pipeline_mode=pl.Buffered(1))


def _side_specs(weights, grid):
    n_steps = 1
    for g in grid:
        n_steps *= g
    specs, shapes = [], []
    for w in weights:
        rows, cols = w.shape
        hold = 1
        while (rows * hold) % n_steps or (rows * hold // n_steps) % V7X_BF16_ROWS:
            hold *= 2
            assert hold <= n_steps, (rows, n_steps)

        def index_map(*idx, hold=hold):
            step = idx[0]
            for g, i in zip(grid[1:], idx[1:]):
                step = step * g + i
            return (step // hold, 0)

        specs.append(pl.BlockSpec((rows * hold // n_steps, cols), index_map))
        shapes.append(jax.ShapeDtypeStruct(w.shape, jnp.bfloat16))
    return specs, shapes


def _side_cast(src_refs, dst_refs):
    for src, dst in zip(src_refs, dst_refs):
        dst[...] = src[...].astype(jnp.bfloat16)


def _swiglu_residual(x, gain, wg_ref, wu_ref, wd_ref, final_gain, final_norm):
    h = _rms_norm(x, gain).astype(jnp.bfloat16)
    gate = jnp.dot(h, wg_ref[...], preferred_element_type=jnp.float32)
    up = jnp.dot(h, wu_ref[...], preferred_element_type=jnp.float32)
    act = (gate * jax.nn.sigmoid(gate) * up).astype(jnp.bfloat16)
    y = jnp.dot(act, wd_ref[...], preferred_element_type=jnp.float32)
    out = x + 0.5 * y
    if final_norm:
        out = _rms_norm(out, final_gain)
    return out


def _load_cast(jobs):
    chunks = []
    ring_pos = {}
    for w_hbm, w_vmem, stage, sem in jobs:
        rows = stage.shape[1]
        for c in range(w_hbm.shape[0] // rows):
            slot = ring_pos.get(id(stage), 0)
            ring_pos[id(stage)] = (slot + 1) % LOAD_DEPTH
            chunks.append((w_hbm, w_vmem, stage, sem, c * rows, rows, slot))

    def copy(k):
        w_hbm, _, stage, sem, row0, rows, slot = chunks[k]
        return pltpu.make_async_copy(w_hbm.at[pl.ds(row0, rows)], stage.at[slot], sem.at[slot])

    for k in range(min(LOAD_DEPTH, len(chunks))):
        copy(k).start()
    for k, (_, w_vmem, stage, _, row0, rows, slot) in enumerate(chunks):
        copy(k).wait()
        w_vmem[pl.ds(row0, rows), :] = stage[slot].astype(jnp.bfloat16)
        if k + LOAD_DEPTH < len(chunks):
            copy(k + LOAD_DEPTH).start()


def _ffn_kernel(*refs, n_side, final_norm, cast_own):
    x_ref, g_ref, wg_ref, wu_ref, wd_ref, gf_ref = refs[:6]
    side_in = refs[6:6 + n_side]
    o_ref = refs[6 + n_side]
    side_out = refs[7 + n_side:7 + 2 * n_side]
    scratch = refs[7 + 2 * n_side:]

    if cast_own:
        wg_bf, wu_bf, wd_bf, stage_in, stage_out, sem_in, sem_out = scratch

        @pl.when(pl.program_id(0) == 0)
        def _():
            _load_cast([(wg_ref, wg_bf, stage_in, sem_in),
                        (wu_ref, wu_bf, stage_in, sem_in),
                        (wd_ref, wd_bf, stage_out, sem_out)])

        wg_ref, wu_ref, wd_ref = wg_bf, wu_bf, wd_bf

    _side_cast(side_in, side_out)
    t_sub = x_ref.shape[0] // FFN_SUB_TILES
    for k in range(FFN_SUB_TILES):
        rows = pl.ds(k * t_sub, t_sub)
        o_ref[rows, :] = _swiglu_residual(x_ref[rows, :], g_ref[...], wg_ref, wu_ref, wd_ref,
                                          gf_ref[...], final_norm)


def _ffn(x2d, gain, w_gate, w_up, w_down, final_gain, side, *, final_norm):
    n_tok, d_model = x2d.shape
    d_ff = w_gate.shape[1]
    assert n_tok % FFN_TOKEN_TILE == 0
    grid = (n_tok // FFN_TOKEN_TILE,)
    cast_own = w_gate.dtype == jnp.float32
    assert w_up.dtype == w_gate.dtype and w_down.dtype == w_gate.dtype
    tile = pl.BlockSpec((FFN_TOKEN_TILE, d_model), lambda i: (i, 0))
    side_specs, side_shapes = _side_specs(side, grid)
    if cast_own:
        assert d_model % (LOAD_CHUNKS * V7X_BF16_ROWS) == 0
        assert d_ff % (LOAD_CHUNKS * V7X_BF16_ROWS) == 0
        hbm = pl.BlockSpec(memory_space=pl.ANY)
        w_specs = [hbm, hbm, hbm]
        scratch = [
            pltpu.VMEM((d_model, d_ff), jnp.bfloat16),
            pltpu.VMEM((d_model, d_ff), jnp.bfloat16),
            pltpu.VMEM((d_ff, d_model), jnp.bfloat16),
            pltpu.VMEM((LOAD_DEPTH, d_model // LOAD_CHUNKS, d_ff), jnp.float32),
            pltpu.VMEM((LOAD_DEPTH, d_ff // LOAD_CHUNKS, d_model), jnp.float32),
            pltpu.SemaphoreType.DMA((LOAD_DEPTH,)),
            pltpu.SemaphoreType.DMA((LOAD_DEPTH,)),
        ]
    else:
        w_specs = [_resident(w_gate.shape), _resident(w_up.shape), _resident(w_down.shape)]
        scratch = []
    outs = pl.pallas_call(
        functools.partial(_ffn_kernel, n_side=len(side), final_norm=final_norm,
                          cast_own=cast_own),
        grid=grid,
        in_specs=[tile, _resident((1, d_model))] + w_specs + [_resident((1, d_model))]
        + side_specs,
        out_specs=[tile] + side_specs,
        out_shape=[jax.ShapeDtypeStruct(x2d.shape, x2d.dtype)] + side_shapes,
        scratch_shapes=scratch,
        compiler_params=pltpu.CompilerParams(
            dimension_semantics=("arbitrary",), vmem_limit_bytes=VMEM_LIMIT_BYTES),
        name="ffn_final" if final_norm else "ffn",
    )(x2d, gain, w_gate, w_up, w_down, final_gain, *side)
    return outs[0], list(outs[1:])


def _mixer_kernel(*refs, n_side):
    x_ref, g_ref, win_ref, cw_ref, pw_ref, ps_ref, wout_ref = refs[:7]
    side_in = refs[7:7 + n_side]
    o_ref = refs[7 + n_side]
    side_out = refs[8 + n_side:8 + 2 * n_side]
    zbuf, ubuf, wcat = refs[8 + 2 * n_side:]
    j = pl.program_id(1)
    t_tile = x_ref.shape[1]
    n_sub = zbuf.shape[0]
    t_sub = t_tile // n_sub
    d_conv = zbuf.shape[2]
    d_pool = ubuf.shape[2]
    n_groups = len(POOL_WINDOWS)
    gc = d_pool // n_groups

    @pl.when((pl.program_id(0) == 0) & (j == 0))
    def _():
        wcat[0:d_conv, :] = wout_ref[0:d_conv, :]
        for g in range(n_groups):
            lo = g * gc
            p_g = (pw_ref[g] * ps_ref[:, lo:lo + gc]).astype(jnp.bfloat16)
            wcat[d_conv + lo:d_conv + lo + gc, :] = jnp.dot(
                p_g, wout_ref[d_conv + lo:d_conv + lo + gc, :],
                preferred_element_type=jnp.float32).astype(jnp.bfloat16)

    @pl.when(j == 0)
    def _():
        zbuf[0, 0:CONV_HALO, :] = jnp.zeros((CONV_HALO, d_conv), jnp.float32)
        ubuf[0, 0:POOL_HALO, :] = jnp.zeros((POOL_HALO, d_pool), jnp.float32)

    _side_cast(side_in, side_out)

    for k in range(n_sub):
        rows = pl.ds(k * t_sub, t_sub)
        nxt = (k + 1) % n_sub
        x = x_ref[0, rows, :]
        h = _rms_norm(x, g_ref[...]).astype(jnp.bfloat16)

        ubuf[k, POOL_HALO:, :] = jnp.dot(h, win_ref[:, 3 * d_conv:],
                                         preferred_element_type=jnp.float32)
        head = POOL_HALO
        head_pos = (j * t_tile + k * t_sub + 1
                    + lax.broadcasted_iota(jnp.int32, (head, gc), 0))
        y_b = []
        for g, w in enumerate(POOL_WINDOWS):
            ue = ubuf[k, :, g * gc:(g + 1) * gc]
            s = ue
            span = 1
            while span < w:
                s = s + pltpu.roll(s, span, axis=0)
                span *= 2
            s = s[POOL_HALO:, :]
            head_count = jnp.minimum(head_pos, w).astype(jnp.float32)
            mean = jnp.concatenate(
                [s[:head, :] / head_count, s[head:, :] * (1.0 / w)], axis=0)
            y_b.append(mean - ue[POOL_HALO:, :])
        ubuf[nxt, 0:POOL_HALO, :] = ubuf[k, t_sub:, :]
        y = jnp.dot(jnp.concatenate(y_b, axis=-1).astype(jnp.bfloat16), wcat[d_conv:, :],
                    preferred_element_type=jnp.float32)

        proj = jnp.dot(h, win_ref[:, 0:3 * d_conv], preferred_element_type=jnp.float32)
        v = proj[:, 0:d_conv]
        gate_b = proj[:, d_conv:2 * d_conv]
        gate_c = proj[:, 2 * d_conv:3 * d_conv]
        zbuf[k, CONV_HALO:, :] = gate_c * v
        ze = zbuf[k]
        cw = cw_ref[...]
        conv = cw[CONV_WIDTH - 1:CONV_WIDTH, :] * ze
        for tap in range(1, CONV_WIDTH):
            conv = conv + (cw[CONV_WIDTH - 1 - tap:CONV_WIDTH - tap, :]
                           * pltpu.roll(ze, tap, axis=0))
        y_a = gate_b * conv[CONV_HALO:, :]
        zbuf[nxt, 0:CONV_HALO, :] = ze[t_sub:, :]
        y = y + jnp.dot(y_a.astype(jnp.bfloat16), wcat[0:d_conv, :],
                        preferred_element_type=jnp.float32)
        o_ref[0, rows, :] = x + y


def _mixer(x, gain, w_in, conv_w, pool_w, pool_scale, w_out, side):
    b, s, d_model = x.shape
    d_conv = conv_w.shape[1]
    d_pool = pool_scale.shape[1]
    assert s % MIXER_TOKEN_TILE == 0 and MIXER_TOKEN_TILE % MIXER_SUB_TILES == 0
    t_sub = MIXER_TOKEN_TILE // MIXER_SUB_TILES
    assert t_sub % V7X_SUBLANES == 0 and t_sub >= POOL_HALO
    grid = (b, s // MIXER_TOKEN_TILE)
    tile = pl.BlockSpec((1, MIXER_TOKEN_TILE, d_model), lambda i, j: (i, j, 0))
    side_specs, side_shapes = _side_specs(side, grid)
    outs = pl.pallas_call(
        functools.partial(_mixer_kernel, n_side=len(side)),
        grid=grid,
        in_specs=[
            tile,
            _resident((1, d_model)),
            _resident(w_in.shape),
            _resident(conv_w.shape),
            _resident(pool_w.shape),
            _resident(pool_scale.shape),
            _resident(w_out.shape),
        ] + side_specs,
        out_specs=[tile] + side_specs,
        out_shape=[jax.ShapeDtypeStruct(x.shape, x.dtype)] + side_shapes,
        scratch_shapes=[
            pltpu.VMEM((MIXER_SUB_TILES, CONV_HALO + t_sub, d_conv), jnp.float32),
            pltpu.VMEM((MIXER_SUB_TILES, POOL_HALO + t_sub, d_pool), jnp.float32),
            pltpu.VMEM(w_out.shape, jnp.bfloat16),
        ],
        compiler_params=pltpu.CompilerParams(
            dimension_semantics=("arbitrary", "arbitrary"),
            vmem_limit_bytes=VMEM_LIMIT_BYTES),
        name="mixer",
    )(x, gain, w_in, conv_w, pool_w, pool_scale, w_out, *side)
    return outs[0], list(outs[1:])


def kernel(x, norm_ffn1, ffn1_w_gate, ffn1_w_up, ffn1_w_down, norm_mix, w_in, conv_w,
           pool_w, pool_scale, w_out, norm_ffn2, ffn2_w_gate, ffn2_w_up, ffn2_w_down,
           norm_final):
    b, s, d_model = x.shape
    depth = norm_ffn1.shape[0]
    assert depth >= 1
    final_gain = norm_final.reshape(1, d_model)
    ffn1_w = [ffn1_w_gate[0], ffn1_w_up[0], ffn1_w_down[0]]
    for l in range(depth):
        last = l == depth - 1
        mix_w = [w_in[l], w_out[l]]
        ffn2_w = [ffn2_w_gate[l], ffn2_w_up[l], ffn2_w_down[l]]
        next_w = [] if last else [ffn1_w_gate[l + 1], ffn1_w_up[l + 1], ffn1_w_down[l + 1]]

        x2d, mix_w = _ffn(x.reshape(b * s, d_model), norm_ffn1[l].reshape(1, d_model),
                          *ffn1_w, final_gain, mix_w, final_norm=False)
        x, ffn2_w = _mixer(x2d.reshape(b, s, d_model), norm_mix[l].reshape(1, d_model),
                           mix_w[0], conv_w[l], pool_w[l], pool_scale[l].reshape(1, -1),
                           mix_w[1], ffn2_w)
        x2d, ffn1_w = _ffn(x.reshape(b * s, d_model), norm_ffn2[l].reshape(1, d_model),
                           *ffn2_w, final_gain, next_w, final_norm=last)
        x = x2d.reshape(b, s, d_model)
    return x
```

```python
import functools

import jax
import jax.numpy as jnp
from jax import lax
from jax.experimental import pallas as pl
from jax.experimental.pallas import tpu as pltpu

EPS = 1e-6
CONV_WIDTH = 3
POOL_WINDOWS = (2, 4, 8, 16)

V7X_SUBLANES = 8
V7X_BF16_ROWS = 16

FFN_TOKEN_TILE = 1024
FFN_SUB_TILES = 4
MIXER_TOKEN_TILE = 1024
MIXER_SUB_TILES = 2
CONV_HALO = V7X_SUBLANES
POOL_HALO = 2 * V7X_SUBLANES
LOAD_CHUNKS = 16
LOAD_DEPTH = 4
VMEM_LIMIT_BYTES = 56 * 1024 * 1024


def _rms_norm(x, gain):
    y = x * lax.rsqrt(jnp.mean(x * x, axis=-1, keepdims=True) + EPS)
    return y * gain


def _resident(shape):
    return pl.BlockSpec(shape, lambda *_: (0,) * len(shape), pipeline_mode=pl.Buffered(1))


def _side_specs(weights, grid):
    n_steps = 1
    for g in grid:
        n_steps *= g
    specs, shapes = [], []
    for w in weights:
        rows, cols = w.shape
        hold = 1
        while (rows * hold) % n_steps or (rows * hold // n_steps) % V7X_BF16_ROWS:
            hold *= 2
            assert hold <= n_steps, (rows, n_steps)

        def index_map(*idx, hold=hold):
            step = idx[0]
            for g, i in zip(grid[1:], idx[1:]):
                step = step * g + i
            return (step // hold, 0)

        specs.append(pl.BlockSpec((rows * hold // n_steps, cols), index_map))
        shapes.append(jax.ShapeDtypeStruct(w.shape, jnp.bfloat16))
    return specs, shapes


def _side_cast(src_refs, dst_refs):
    for src, dst in zip(src_refs, dst_refs):
        dst[...] = src[...].astype(jnp.bfloat16)


def _swiglu_residual(x, gain, wg_ref, wu_ref, wd_ref, final_gain, final_norm):
    h = _rms_norm(x, gain).astype(jnp.bfloat16)
    gate = jnp.dot(h, wg_ref[...], preferred_element_type=jnp.float32)
    up = jnp.dot(h, wu_ref[...], preferred_element_type=jnp.float32)
    act = (gate * jax.nn.sigmoid(gate) * up).astype(jnp.bfloat16)
    y = jnp.dot(act, wd_ref[...], preferred_element_type=jnp.float32)
    out = x + 0.5 * y
    if final_norm:
        out = _rms_norm(out, final_gain)
    return out


def _load_cast(jobs):
    chunks = []
    ring_pos = {}
    for w_hbm, w_vmem, stage, sem in jobs:
        rows = stage.shape[1]
        for c in range(w_hbm.shape[0] // rows):
            slot = ring_pos.get(id(stage), 0)
            ring_pos[id(stage)] = (slot + 1) % LOAD_DEPTH
            chunks.append((w_hbm, w_vmem, stage, sem, c * rows, rows, slot))

    def copy(k):
        w_hbm, _, stage, sem, row0, rows, slot = chunks[k]
        return pltpu.make_async_copy(w_hbm.at[pl.ds(row0, rows)], stage.at[slot], sem.at[slot])

    for k in range(min(LOAD_DEPTH, len(chunks))):
        copy(k).start()
    for k, (_, w_vmem, stage, _, row0, rows, slot) in enumerate(chunks):
        copy(k).wait()
        w_vmem[pl.ds(row0, rows), :] = stage[slot].astype(jnp.bfloat16)
        if k + LOAD_DEPTH < len(chunks):
            copy(k + LOAD_DEPTH).start()


def _ffn_kernel(*refs, n_side, final_norm, cast_own):
    x_ref, g_ref, wg_ref, wu_ref, wd_ref, gf_ref = refs[:6]
    side_in = refs[6:6 + n_side]
    o_ref = refs[6 + n_side]
    side_out = refs[7 + n_side:7 + 2 * n_side]
    scratch = refs[7 + 2 * n_side:]

    if cast_own:
        wg_bf, wu_bf, wd_bf, stage_in, stage_out, sem_in, sem_out = scratch

        @pl.when(pl.program_id(0) == 0)
        def _():
            _load_cast([(wg_ref, wg_bf, stage_in, sem_in),
                        (wu_ref, wu_bf, stage_in, sem_in),
                        (wd_ref, wd_bf, stage_out, sem_out)])

        wg_ref, wu_ref, wd_ref = wg_bf, wu_bf, wd_bf

    _side_cast(side_in, side_out)
    t_sub = x_ref.shape[0] // FFN_SUB_TILES
    for k in range(FFN_SUB_TILES):
        rows = pl.ds(k * t_sub, t_sub)
        o_ref[rows, :] = _swiglu_residual(x_ref[rows, :], g_ref[...], wg_ref, wu_ref, wd_ref,
                                          gf_ref[...], final_norm)


def _ffn(x2d, gain, w_gate, w_up, w_down, final_gain, side, *, final_norm):
    n_tok, d_model = x2d.shape
    d_ff = w_gate.shape[1]
    assert n_tok % FFN_TOKEN_TILE == 0
    grid = (n_tok // FFN_TOKEN_TILE,)
    cast_own = w_gate.dtype == jnp.float32
    assert w_up.dtype == w_gate.dtype and w_down.dtype == w_gate.dtype
    tile = pl.BlockSpec((FFN_TOKEN_TILE, d_model), lambda i: (i, 0))
    side_specs, side_shapes = _side_specs(side, grid)
    if cast_own:
        assert d_model % (LOAD_CHUNKS * V7X_BF16_ROWS) == 0
        assert d_ff % (LOAD_CHUNKS * V7X_BF16_ROWS) == 0
        hbm = pl.BlockSpec(memory_space=pl.ANY)
        w_specs = [hbm, hbm, hbm]
        scratch = [
            pltpu.VMEM((d_model, d_ff), jnp.bfloat16),
            pltpu.VMEM((d_model, d_ff), jnp.bfloat16),
            pltpu.VMEM((d_ff, d_model), jnp.bfloat16),
            pltpu.VMEM((LOAD_DEPTH, d_model // LOAD_CHUNKS, d_ff), jnp.float32),
            pltpu.VMEM((LOAD_DEPTH, d_ff // LOAD_CHUNKS, d_model), jnp.float32),
            pltpu.SemaphoreType.DMA((LOAD_DEPTH,)),
            pltpu.SemaphoreType.DMA((LOAD_DEPTH,)),
        ]
    else:
        w_specs = [_resident(w_gate.shape), _resident(w_up.shape), _resident(w_down.shape)]
        scratch = []
    outs = pl.pallas_call(
        functools.partial(_ffn_kernel, n_side=len(side), final_norm=final_norm,
                          cast_own=cast_own),
        grid=grid,
        in_specs=[tile, _resident((1, d_model))] + w_specs + [_resident((1, d_model))]
        + side_specs,
        out_specs=[tile] + side_specs,
        out_shape=[jax.ShapeDtypeStruct(x2d.shape, x2d.dtype)] + side_shapes,
        scratch_shapes=scratch,
        compiler_params=pltpu.CompilerParams(
            dimension_semantics=("arbitrary",), vmem_limit_bytes=VMEM_LIMIT_BYTES),
        name="ffn_final" if final_norm else "ffn",
    )(x2d, gain, w_gate, w_up, w_down, final_gain, *side)
    return outs[0], list(outs[1:])


def _mixer_kernel(*refs, n_side, final_norm):
    x_ref, g_ref, win_ref, cw_ref, pw_ref, ps_ref, wout_ref = refs[:7]
    g2_ref, wg_ref, wu_ref, wd_ref, gf_ref = refs[7:12]
    side_in = refs[12:12 + n_side]
    o_ref = refs[12 + n_side]
    side_out = refs[13 + n_side:13 + 2 * n_side]
    zbuf, ubuf, wcat = refs[13 + 2 * n_side:]
    j = pl.program_id(1)
    t_tile = x_ref.shape[1]
    n_sub = zbuf.shape[0]
    t_sub = t_tile // n_sub
    d_conv = zbuf.shape[2]
    d_pool = ubuf.shape[2]
    n_groups = len(POOL_WINDOWS)
    gc = d_pool // n_groups

    @pl.when((pl.program_id(0) == 0) & (j == 0))
    def _():
        wcat[0:d_conv, :] = wout_ref[0:d_conv, :]
        for g in range(n_groups):
            lo = g * gc
            p_g = (pw_ref[g] * ps_ref[:, lo:lo + gc]).astype(jnp.bfloat16)
            wcat[d_conv + lo:d_conv + lo + gc, :] = jnp.dot(
                p_g, wout_ref[d_conv + lo:d_conv + lo + gc, :],
                preferred_element_type=jnp.float32).astype(jnp.bfloat16)

    @pl.when(j == 0)
    def _():
        zbuf[0, 0:CONV_HALO, :] = jnp.zeros((CONV_HALO, d_conv), jnp.float32)
        ubuf[0, 0:POOL_HALO, :] = jnp.zeros((POOL_HALO, d_pool), jnp.float32)

    _side_cast(side_in, side_out)

    mixed = []
    for k in range(n_sub):
        rows = pl.ds(k * t_sub, t_sub)
        nxt = (k + 1) % n_sub
        x = x_ref[0, rows, :]
        h = _rms_norm(x, g_ref[...]).astype(jnp.bfloat16)

        ubuf[k, POOL_HALO:, :] = jnp.dot(h, win_ref[:, 3 * d_conv:],
                                         preferred_element_type=jnp.float32)
        head = POOL_HALO
        head_pos = (j * t_tile + k * t_sub + 1
                    + lax.broadcasted_iota(jnp.int32, (head, gc), 0))
        y_b = []
        for g, w in enumerate(POOL_WINDOWS):
            ue = ubuf[k, :, g * gc:(g + 1) * gc]
            s = ue
            span = 1
            while span < w:
                s = s + pltpu.roll(s, span, axis=0)
                span *= 2
            s = s[POOL_HALO:, :]
            head_count = jnp.minimum(head_pos, w).astype(jnp.float32)
            mean = jnp.concatenate(
                [s[:head, :] / head_count, s[head:, :] * (1.0 / w)], axis=0)
            y_b.append(mean - ue[POOL_HALO:, :])
        ubuf[nxt, 0:POOL_HALO, :] = ubuf[k, t_sub:, :]
        y = jnp.dot(jnp.concatenate(y_b, axis=-1).astype(jnp.bfloat16), wcat[d_conv:, :],
                    preferred_element_type=jnp.float32)

        proj = jnp.dot(h, win_ref[:, 0:3 * d_conv], preferred_element_type=jnp.float32)
        v = proj[:, 0:d_conv]
        gate_b = proj[:, d_conv:2 * d_conv]
        gate_c = proj[:, 2 * d_conv:3 * d_conv]
        zbuf[k, CONV_HALO:, :] = gate_c * v
        ze = zbuf[k]
        cw = cw_ref[...]
        conv = cw[CONV_WIDTH - 1:CONV_WIDTH, :] * ze
        for tap in range(1, CONV_WIDTH):
            conv = conv + (cw[CONV_WIDTH - 1 - tap:CONV_WIDTH - tap, :]
                           * pltpu.roll(ze, tap, axis=0))
        y_a = gate_b * conv[CONV_HALO:, :]
        zbuf[nxt, 0:CONV_HALO, :] = ze[t_sub:, :]
        y = y + jnp.dot(y_a.astype(jnp.bfloat16), wcat[0:d_conv, :],
                        preferred_element_type=jnp.float32)
        mixed.append(x + y)

    f_sub = t_tile // FFN_SUB_TILES
    per_block = t_sub // f_sub
    for k in range(FFN_SUB_TILES):
        r0 = (k % per_block) * f_sub
        o_ref[0, pl.ds(k * f_sub, f_sub), :] = _swiglu_residual(
            mixed[k // per_block][r0:r0 + f_sub, :], g2_ref[...], wg_ref, wu_ref, wd_ref,
            gf_ref[...], final_norm)


def _mixer_ffn(x, gain, w_in, conv_w, pool_w, pool_scale, w_out, gain2, w_gate, w_up, w_down,
               final_gain, side, *, final_norm):
    b, s, d_model = x.shape
    d_conv = conv_w.shape[1]
    d_pool = pool_scale.shape[1]
    assert s % MIXER_TOKEN_TILE == 0 and MIXER_TOKEN_TILE % MIXER_SUB_TILES == 0
    t_sub = MIXER_TOKEN_TILE // MIXER_SUB_TILES
    assert t_sub % V7X_SUBLANES == 0 and t_sub >= POOL_HALO
    assert FFN_SUB_TILES % MIXER_SUB_TILES == 0
    grid = (b, s // MIXER_TOKEN_TILE)
    tile = pl.BlockSpec((1, MIXER_TOKEN_TILE, d_model), lambda i, j: (i, j, 0))
    side_specs, side_shapes = _side_specs(side, grid)
    outs = pl.pallas_call(
        functools.partial(_mixer_kernel, n_side=len(side), final_norm=final_norm),
        grid=grid,
        in_specs=[
            tile,
            _resident((1, d_model)),
            _resident(w_in.shape),
            _resident(conv_w.shape),
            _resident(pool_w.shape),
            _resident(pool_scale.shape),
            _resident(w_out.shape),
            _resident((1, d_model)),
            _resident(w_gate.shape),
            _resident(w_up.shape),
            _resident(w_down.shape),
            _resident((1, d_model)),
        ] + side_specs,
        out_specs=[tile] + side_specs,
        out_shape=[jax.ShapeDtypeStruct(x.shape, x.dtype)] + side_shapes,
        scratch_shapes=[
            pltpu.VMEM((MIXER_SUB_TILES, CONV_HALO + t_sub, d_conv), jnp.float32),
            pltpu.VMEM((MIXER_SUB_TILES, POOL_HALO + t_sub, d_pool), jnp.float32),
            pltpu.VMEM(w_out.shape, jnp.bfloat16),
        ],
        compiler_params=pltpu.CompilerParams(
            dimension_semantics=("arbitrary", "arbitrary"),
            vmem_limit_bytes=VMEM_LIMIT_BYTES),
        name="mixer_ffn",
    )(x, gain, w_in, conv_w, pool_w, pool_scale, w_out, gain2, w_gate, w_up, w_down,
      final_gain, *side)
    return outs[0], list(outs[1:])


def kernel(x, norm_ffn1, ffn1_w_gate, ffn1_w_up, ffn1_w_down, norm_mix, w_in, conv_w,
           pool_w, pool_scale, w_out, norm_ffn2, ffn2_w_gate, ffn2_w_up, ffn2_w_down,
           norm_final):
    b, s, d_model = x.shape
    depth = norm_ffn1.shape[0]
    assert depth >= 1
    final_gain = norm_final.reshape(1, d_model)
    ffn1_w = [ffn1_w_gate[0], ffn1_w_up[0], ffn1_w_down[0]]
    for l in range(depth):
        last = l == depth - 1
        later_w = [w_in[l], w_out[l], ffn2_w_gate[l], ffn2_w_up[l], ffn2_w_down[l]]
        next_w = [] if last else [ffn1_w_gate[l + 1], ffn1_w_up[l + 1], ffn1_w_down[l + 1]]

        x2d, later_w = _ffn(x.reshape(b * s, d_model), norm_ffn1[l].reshape(1, d_model),
                            *ffn1_w, final_gain, later_w, final_norm=False)
        x, ffn1_w = _mixer_ffn(x2d.reshape(b, s, d_model), norm_mix[l].reshape(1, d_model),
                               later_w[0], conv_w[l], pool_w[l], pool_scale[l].reshape(1, -1),
                               later_w[1], norm_ffn2[l].reshape(1, d_model), *later_w[2:],
                               final_gain, next_w, final_norm=last)
    return x
```

```python
import functools

import jax
import jax.numpy as jnp
from jax import lax
from jax.experimental import pallas as pl
from jax.experimental.pallas import tpu as pltpu

EPS = 1e-6
CONV_WIDTH = 3
POOL_WINDOWS = (2, 4, 8, 16)

V7X_SUBLANES = 8
V7X_BF16_ROWS = 16
V7X_VMEM_BYTES = 64 * 1024 * 1024

FFN_TOKEN_TILE = 1024
FFN_SUB_TILES = 4
MIXER_TOKEN_TILE = 1024
MIXER_SUB_TILES = 2
CONV_HALO = V7X_SUBLANES
POOL_HALO = 2 * V7X_SUBLANES
LOAD_CHUNKS = 16
LOAD_DEPTH = 7
VMEM_LIMIT_BYTES = V7X_VMEM_BYTES * 7 // 8


def _rms_norm(x, gain):
    y = x * lax.rsqrt(jnp.mean(x * x, axis=-1, keepdims=True) + EPS)
    return y * gain


def _resident(shape):
    return pl.BlockSpec(shape, lambda *_: (0,) * len(shape), pipeline_mode=pl.Buffered(1))


def _side_specs(weights, grid):
    n_steps = 1
    for g in grid:
        n_steps *= g
    specs, shapes = [], []
    for w in weights:
        rows, cols = w.shape
        hold = 1
        while (rows * hold) % n_steps or (rows * hold // n_steps) % V7X_BF16_ROWS:
            hold *= 2
            assert hold <= n_steps, (rows, n_steps)

        def index_map(*idx, hold=hold):
            step = idx[0]
            for g, i in zip(grid[1:], idx[1:]):
                step = step * g + i
            return (step // hold, 0)

        specs.append(pl.BlockSpec((rows * hold // n_steps, cols), index_map))
        shapes.append(jax.ShapeDtypeStruct(w.shape, jnp.bfloat16))
    return specs, shapes


def _side_cast(src_refs, dst_refs):
    for src, dst in zip(src_refs, dst_refs):
        dst[...] = src[...].astype(jnp.bfloat16)


def _swiglu_residual(x, gain, wg_ref, wu_ref, wd_ref, final_gain, final_norm):
    h = _rms_norm(x, gain).astype(jnp.bfloat16)
    gate = jnp.dot(h, wg_ref[...], preferred_element_type=jnp.float32)
    up = jnp.dot(h, wu_ref[...], preferred_element_type=jnp.float32)
    act = (gate * jax.nn.sigmoid(gate) * up).astype(jnp.bfloat16)
    y = jnp.dot(act, wd_ref[...], preferred_element_type=jnp.float32)
    out = x + 0.5 * y
    if final_norm:
        out = _rms_norm(out, final_gain)
    return out


def _load_cast(jobs):
    chunks = []
    ring_pos = {}
    for w_hbm, w_vmem, stage, sem in jobs:
        rows = stage.shape[1]
        for c in range(w_hbm.shape[0] // rows):
            slot = ring_pos.get(id(stage), 0)
            ring_pos[id(stage)] = (slot + 1) % LOAD_DEPTH
            chunks.append((w_hbm, w_vmem, stage, sem, c * rows, rows, slot))

    def copy(k):
        w_hbm, _, stage, sem, row0, rows, slot = chunks[k]
        return pltpu.make_async_copy(w_hbm.at[pl.ds(row0, rows)], stage.at[slot], sem.at[slot])

    for k in range(min(LOAD_DEPTH, len(chunks))):
        copy(k).start()
    for k, (_, w_vmem, stage, _, row0, rows, slot) in enumerate(chunks):
        copy(k).wait()
        w_vmem[pl.ds(row0, rows), :] = stage[slot].astype(jnp.bfloat16)
        if k + LOAD_DEPTH < len(chunks):
            copy(k + LOAD_DEPTH).start()


def _ffn_kernel(*refs, n_side, final_norm, cast_own):
    x_ref, g_ref, wg_ref, wu_ref, wd_ref, gf_ref = refs[:6]
    side_in = refs[6:6 + n_side]
    o_ref = refs[6 + n_side]
    side_out = refs[7 + n_side:7 + 2 * n_side]
    scratch = refs[7 + 2 * n_side:]

    if cast_own:
        wg_bf, wu_bf, wd_bf, stage_in, stage_out, sem_in, sem_out = scratch

        @pl.when(pl.program_id(0) == 0)
        def _():
            _load_cast([(wg_ref, wg_bf, stage_in, sem_in),
                        (wu_ref, wu_bf, stage_in, sem_in),
                        (wd_ref, wd_bf, stage_out, sem_out)])

        wg_ref, wu_ref, wd_ref = wg_bf, wu_bf, wd_bf

    _side_cast(side_in, side_out)
    t_sub = x_ref.shape[0] // FFN_SUB_TILES
    for k in range(FFN_SUB_TILES):
        rows = pl.ds(k * t_sub, t_sub)
        o_ref[rows, :] = _swiglu_residual(x_ref[rows, :], g_ref[...], wg_ref, wu_ref, wd_ref,
                                          gf_ref[...], final_norm)


def _ffn(x2d, gain, w_gate, w_up, w_down, final_gain, side, *, final_norm):
    n_tok, d_model = x2d.shape
    d_ff = w_gate.shape[1]
    assert n_tok % FFN_TOKEN_TILE == 0
    grid = (n_tok // FFN_TOKEN_TILE,)
    cast_own = w_gate.dtype == jnp.float32
    assert w_up.dtype == w_gate.dtype and w_down.dtype == w_gate.dtype
    tile = pl.BlockSpec((FFN_TOKEN_TILE, d_model), lambda i: (i, 0))
    side_specs, side_shapes = _side_specs(side, grid)
    if cast_own:
        assert d_model % (LOAD_CHUNKS * V7X_BF16_ROWS) == 0
        assert d_ff % (LOAD_CHUNKS * V7X_BF16_ROWS) == 0
        hbm = pl.BlockSpec(memory_space=pl.ANY)
        w_specs = [hbm, hbm, hbm]
        scratch = [
            pltpu.VMEM((d_model, d_ff), jnp.bfloat16),
            pltpu.VMEM((d_model, d_ff), jnp.bfloat16),
            pltpu.VMEM((d_ff, d_model), jnp.bfloat16),
            pltpu.VMEM((LOAD_DEPTH, d_model // LOAD_CHUNKS, d_ff), jnp.float32),
            pltpu.VMEM((LOAD_DEPTH, d_ff // LOAD_CHUNKS, d_model), jnp.float32),
            pltpu.SemaphoreType.DMA((LOAD_DEPTH,)),
            pltpu.SemaphoreType.DMA((LOAD_DEPTH,)),
        ]
    else:
        w_specs = [_resident(w_gate.shape), _resident(w_up.shape), _resident(w_down.shape)]
        scratch = []
    outs = pl.pallas_call(
        functools.partial(_ffn_kernel, n_side=len(side), final_norm=final_norm,
                          cast_own=cast_own),
        grid=grid,
        in_specs=[tile, _resident((1, d_model))] + w_specs + [_resident((1, d_model))]
        + side_specs,
        out_specs=[tile] + side_specs,
        out_shape=[jax.ShapeDtypeStruct(x2d.shape, x2d.dtype)] + side_shapes,
        scratch_shapes=scratch,
        compiler_params=pltpu.CompilerParams(
            dimension_semantics=("arbitrary",), vmem_limit_bytes=VMEM_LIMIT_BYTES),
        name="ffn_final" if final_norm else "ffn",
    )(x2d, gain, w_gate, w_up, w_down, final_gain, *side)
    return outs[0], list(outs[1:])


def _mixer_kernel(*refs, n_side):
    x_ref, g_ref, win_ref, cw_ref, pw_ref, ps_ref, wout_ref = refs[:7]
    side_in = refs[7:7 + n_side]
    o_ref = refs[7 + n_side]
    side_out = refs[8 + n_side:8 + 2 * n_side]
    zbuf, ubuf, wcat = refs[8 + 2 * n_side:]
    j = pl.program_id(1)
    t_tile = x_ref.shape[1]
    n_sub = zbuf.shape[0]
    t_sub = t_tile // n_sub
    d_conv = zbuf.shape[2]
    d_pool = ubuf.shape[2]
    n_groups = len(POOL_WINDOWS)
    gc = d_pool // n_groups

    @pl.when((pl.program_id(0) == 0) & (j == 0))
    def _():
        wcat[0:d_conv, :] = wout_ref[0:d_conv, :]
        for g in range(n_groups):
            lo = g * gc
            p_g = (pw_ref[g] * ps_ref[:, lo:lo + gc]).astype(jnp.bfloat16)
            wcat[d_conv + lo:d_conv + lo + gc, :] = jnp.dot(
                p_g, wout_ref[d_conv + lo:d_conv + lo + gc, :],
                preferred_element_type=jnp.float32).astype(jnp.bfloat16)

    @pl.when(j == 0)
    def _():
        zbuf[0, 0:CONV_HALO, :] = jnp.zeros((CONV_HALO, d_conv), jnp.float32)
        ubuf[0, 0:POOL_HALO, :] = jnp.zeros((POOL_HALO, d_pool), jnp.float32)

    _side_cast(side_in, side_out)

    for k in range(n_sub):
        rows = pl.ds(k * t_sub, t_sub)
        nxt = (k + 1) % n_sub
        x = x_ref[0, rows, :]
        h = _rms_norm(x, g_ref[...]).astype(jnp.bfloat16)

        ubuf[k, POOL_HALO:, :] = jnp.dot(h, win_ref[:, 3 * d_conv:],
                                         preferred_element_type=jnp.float32)
        head = POOL_HALO
        head_pos = (j * t_tile + k * t_sub + 1
                    + lax.broadcasted_iota(jnp.int32, (head, gc), 0))
        y_b = []
        for g, w in enumerate(POOL_WINDOWS):
            ue = ubuf[k, :, g * gc:(g + 1) * gc]
            s = ue
            span = 1
            while span < w:
                s = s + pltpu.roll(s, span, axis=0)
                span *= 2
            s = s[POOL_HALO:, :]
            head_count = jnp.minimum(head_pos, w).astype(jnp.float32)
            mean = jnp.concatenate(
                [s[:head, :] / head_count, s[head:, :] * (1.0 / w)], axis=0)
            y_b.append(mean - ue[POOL_HALO:, :])
        ubuf[nxt, 0:POOL_HALO, :] = ubuf[k, t_sub:, :]
        y = jnp.dot(jnp.concatenate(y_b, axis=-1).astype(jnp.bfloat16), wcat[d_conv:, :],
                    preferred_element_type=jnp.float32)

        proj = jnp.dot(h, win_ref[:, 0:3 * d_conv], preferred_element_type=jnp.float32)
        v = proj[:, 0:d_conv]
        gate_b = proj[:, d_conv:2 * d_conv]
        gate_c = proj[:, 2 * d_conv:3 * d_conv]
        zbuf[k, CONV_HALO:, :] = gate_c * v
        ze = zbuf[k]
        cw = cw_ref[...]
        conv = cw[CONV_WIDTH - 1:CONV_WIDTH, :] * ze
        for tap in range(1, CONV_WIDTH):
            conv = conv + (cw[CONV_WIDTH - 1 - tap:CONV_WIDTH - tap, :]
                           * pltpu.roll(ze, tap, axis=0))
        y_a = gate_b * conv[CONV_HALO:, :]
        zbuf[nxt, 0:CONV_HALO, :] = ze[t_sub:, :]
        y = y + jnp.dot(y_a.astype(jnp.bfloat16), wcat[0:d_conv, :],
                        preferred_element_type=jnp.float32)
        o_ref[0, rows, :] = x + y


def _mixer(x, gain, w_in, conv_w, pool_w, pool_scale, w_out, side):
    b, s, d_model = x.shape
    d_conv = conv_w.shape[1]
    d_pool = pool_scale.shape[1]
    assert s % MIXER_TOKEN_TILE == 0 and MIXER_TOKEN_TILE % MIXER_SUB_TILES == 0
    t_sub = MIXER_TOKEN_TILE // MIXER_SUB_TILES
    assert t_sub % V7X_SUBLANES == 0 and t_sub >= POOL_HALO
    grid = (b, s // MIXER_TOKEN_TILE)
    tile = pl.BlockSpec((1, MIXER_TOKEN_TILE, d_model), lambda i, j: (i, j, 0))
    side_specs, side_shapes = _side_specs(side, grid)
    outs = pl.pallas_call(
        functools.partial(_mixer_kernel, n_side=len(side)),
        grid=grid,
        in_specs=[
            tile,
            _resident((1, d_model)),
            _resident(w_in.shape),
            _resident(conv_w.shape),
            _resident(pool_w.shape),
            _resident(pool_scale.shape),
            _resident(w_out.shape),
        ] + side_specs,
        out_specs=[tile] + side_specs,
        out_shape=[jax.ShapeDtypeStruct(x.shape, x.dtype)] + side_shapes,
        scratch_shapes=[
            pltpu.VMEM((MIXER_SUB_TILES, CONV_HALO + t_sub, d_conv), jnp.float32),
            pltpu.VMEM((MIXER_SUB_TILES, POOL_HALO + t_sub, d_pool), jnp.float32),
            pltpu.VMEM(w_out.shape, jnp.bfloat16),
        ],
        compiler_params=pltpu.CompilerParams(
            dimension_semantics=("arbitrary", "arbitrary"),
            vmem_limit_bytes=VMEM_LIMIT_BYTES),
        name="mixer",
    )(x, gain, w_in, conv_w, pool_w, pool_scale, w_out, *side)
    return outs[0], list(outs[1:])


def kernel(x, norm_ffn1, ffn1_w_gate, ffn1_w_up, ffn1_w_down, norm_mix, w_in, conv_w,
           pool_w, pool_scale, w_out, norm_ffn2, ffn2_w_gate, ffn2_w_up, ffn2_w_down,
           norm_final):
    b, s, d_model = x.shape
    depth = norm_ffn1.shape[0]
    assert depth >= 1
    final_gain = norm_final.reshape(1, d_model)
    ffn1_w = [ffn1_w_gate[0], ffn1_w_up[0], ffn1_w_down[0]]
    for l in range(depth):
        last = l == depth - 1
        mix_w = [w_in[l], w_out[l]]
        ffn2_w = [ffn2_w_gate[l], ffn2_w_up[l], ffn2_w_down[l]]
        next_w = [] if last else [ffn1_w_gate[l + 1], ffn1_w_up[l + 1], ffn1_w_down[l + 1]]

        x2d, mix_w = _ffn(x.reshape(b * s, d_model), norm_ffn1[l].reshape(1, d_model),
                          *ffn1_w, final_gain, mix_w, final_norm=False)
        x, ffn2_w = _mixer(x2d.reshape(b, s, d_model), norm_mix[l].reshape(1, d_model),
                           mix_w[0], conv_w[l], pool_w[l], pool_scale[l].reshape(1, -1),
                           mix_w[1], ffn2_w)
        x2d, ffn1_w = _ffn(x.reshape(b * s, d_model), norm_ffn2[l].reshape(1, d_model),
                           *ffn2_w, final_gain, next_w, final_norm=last)
        x = x2d.reshape(b, s, d_model)
    return x
```

```python
import functools

import jax
import jax.numpy as jnp
from jax import lax
from jax.experimental import pallas as pl
from jax.experimental.pallas import tpu as pltpu

EPS = 1e-6
CONV_WIDTH = 3
POOL_WINDOWS = (2, 4, 8, 16)

V7X_SUBLANES = 8
V7X_BF16_ROWS = 16
V7X_VMEM_BYTES = 64 * 1024 * 1024

FFN_TOKEN_TILE = 1024
FFN_SUB_TILES = 4
MIXER_TOKEN_TILE = 1024
MIXER_SUB_TILES = 2
CONV_HALO = V7X_SUBLANES
POOL_HALO = 2 * V7X_SUBLANES
LOAD_CHUNKS = 16
LOAD_DEPTH = 7
VMEM_LIMIT_BYTES = V7X_VMEM_BYTES * 7 // 8


def _rms_norm(x, gain):
    y = x * lax.rsqrt(jnp.mean(x * x, axis=-1, keepdims=True) + EPS)
    return y * gain


def _resident(shape):
    return pl.BlockSpec(shape, lambda *_: (0,) * len(shape), pipeline_mode=pl.Buffered(1))


def _side_specs(weights, grid):
    n_steps = 1
    for g in grid:
        n_steps *= g
    specs, shapes = [], []
    for w in weights:
        rows, cols = w.shape
        hold = 1
        while (rows * hold) % n_steps or (rows * hold // n_steps) % V7X_BF16_ROWS:
            hold *= 2
            assert hold <= n_steps, (rows, n_steps)

        def index_map(*idx, hold=hold):
            step = idx[0]
            for g, i in zip(grid[1:], idx[1:]):
                step = step * g + i
            return (step // hold, 0)

        specs.append(pl.BlockSpec((rows * hold // n_steps, cols), index_map))
        shapes.append(jax.ShapeDtypeStruct(w.shape, jnp.bfloat16))
    return specs, shapes


def _side_cast(src_refs, dst_refs):
    for src, dst in zip(src_refs, dst_refs):
        dst[...] = src[...].astype(jnp.bfloat16)


def _swiglu_residual(x, gain, wg_ref, wu_ref, wd_ref, final_gain, final_norm):
    h = _rms_norm(x, gain).astype(jnp.bfloat16)
    gate = jnp.dot(h, wg_ref[...], preferred_element_type=jnp.float32)
    up = jnp.dot(h, wu_ref[...], preferred_element_type=jnp.float32)
    act = (gate * jax.nn.sigmoid(gate) * up).astype(jnp.bfloat16)
    y = jnp.dot(act, wd_ref[...], preferred_element_type=jnp.float32)
    out = x + 0.5 * y
    if final_norm:
        out = _rms_norm(out, final_gain)
    return out


def _load_cast(jobs):
    chunks = []
    ring_pos = {}
    for w_hbm, w_vmem, stage, sem in jobs:
        rows = stage.shape[1]
        for c in range(w_hbm.shape[0] // rows):
            slot = ring_pos.get(id(stage), 0)
            ring_pos[id(stage)] = (slot + 1) % LOAD_DEPTH
            chunks.append((w_hbm, w_vmem, stage, sem, c * rows, rows, slot))

    def copy(k):
        w_hbm, _, stage, sem, row0, rows, slot = chunks[k]
        return pltpu.make_async_copy(w_hbm.at[pl.ds(row0, rows)], stage.at[slot], sem.at[slot])

    for k in range(min(LOAD_DEPTH, len(chunks))):
        copy(k).start()
    for k, (_, w_vmem, stage, _, row0, rows, slot) in enumerate(chunks):
        copy(k).wait()
        w_vmem[pl.ds(row0, rows), :] = stage[slot].astype(jnp.bfloat16)
        if k + LOAD_DEPTH < len(chunks):
            copy(k + LOAD_DEPTH).start()


def _ffn_kernel(*refs, n_side, final_norm, cast_own):
    x_ref, g_ref, wg_ref, wu_ref, wd_ref, gf_ref = refs[:6]
    side_in = refs[6:6 + n_side]
    o_ref = refs[6 + n_side]
    side_out = refs[7 + n_side:7 + 2 * n_side]
    scratch = refs[7 + 2 * n_side:]

    if cast_own:
        wg_bf, wu_bf, wd_bf, stage_in, stage_out, sem_in, sem_out = scratch

        @pl.when(pl.program_id(0) == 0)
        def _():
            _load_cast([(wg_ref, wg_bf, stage_in, sem_in),
                        (wu_ref, wu_bf, stage_in, sem_in),
                        (wd_ref, wd_bf, stage_out, sem_out)])

        wg_ref, wu_ref, wd_ref = wg_bf, wu_bf, wd_bf

    _side_cast(side_in, side_out)
    t_sub = x_ref.shape[0] // FFN_SUB_TILES
    for k in range(FFN_SUB_TILES):
        rows = pl.ds(k * t_sub, t_sub)
        o_ref[rows, :] = _swiglu_residual(x_ref[rows, :], g_ref[...], wg_ref, wu_ref, wd_ref,
                                          gf_ref[...], final_norm)


def _ffn(x2d, gain, w_gate, w_up, w_down, final_gain, side, *, final_norm):
    n_tok, d_model = x2d.shape
    d_ff = w_gate.shape[1]
    assert n_tok % FFN_TOKEN_TILE == 0
    grid = (n_tok // FFN_TOKEN_TILE,)
    cast_own = w_gate.dtype == jnp.float32
    assert w_up.dtype == w_gate.dtype and w_down.dtype == w_gate.dtype
    tile = pl.BlockSpec((FFN_TOKEN_TILE, d_model), lambda i: (i, 0))
    side_specs, side_shapes = _side_specs(side, grid)
    if cast_own:
        assert d_model % (LOAD_CHUNKS * V7X_BF16_ROWS) == 0
        assert d_ff % (LOAD_CHUNKS * V7X_BF16_ROWS) == 0
        hbm = pl.BlockSpec(memory_space=pl.ANY)
        w_specs = [hbm, hbm, hbm]
        scratch = [
            pltpu.VMEM((d_model, d_ff), jnp.bfloat16),
            pltpu.VMEM((d_model, d_ff), jnp.bfloat16),
            pltpu.VMEM((d_ff, d_model), jnp.bfloat16),
            pltpu.VMEM((LOAD_DEPTH, d_model // LOAD_CHUNKS, d_ff), jnp.float32),
            pltpu.VMEM((LOAD_DEPTH, d_ff // LOAD_CHUNKS, d_model), jnp.float32),
            pltpu.SemaphoreType.DMA((LOAD_DEPTH,)),
            pltpu.SemaphoreType.DMA((LOAD_DEPTH,)),
        ]
    else:
        w_specs = [_resident(w_gate.shape), _resident(w_up.shape), _resident(w_down.shape)]
        scratch = []
    outs = pl.pallas_call(
        functools.partial(_ffn_kernel, n_side=len(side), final_norm=final_norm,
                          cast_own=cast_own),
        grid=grid,
        in_specs=[tile, _resident((1, d_model))] + w_specs + [_resident((1, d_model))]
        + side_specs,
        out_specs=[tile] + side_specs,
        out_shape=[jax.ShapeDtypeStruct(x2d.shape, x2d.dtype)] + side_shapes,
        scratch_shapes=scratch,
        compiler_params=pltpu.CompilerParams(
            dimension_semantics=("arbitrary",), vmem_limit_bytes=VMEM_LIMIT_BYTES),
        name="ffn_final" if final_norm else "ffn",
    )(x2d, gain, w_gate, w_up, w_down, final_gain, *side)
    return outs[0], list(outs[1:])


def _mixer_kernel(*refs, n_side):
    x_ref, g_ref, win_ref, cw_ref, pw_ref, ps_ref, wout_ref = refs[:7]
    side_in = refs[7:7 + n_side]
    o_ref = refs[7 + n_side]
    side_out = refs[8 + n_side:8 + 2 * n_side]
    zbuf, ubuf, wcat = refs[8 + 2 * n_side:]
    j = pl.program_id(1)
    t_tile = x_ref.shape[1]
    n_sub = zbuf.shape[0]
    t_sub = t_tile // n_sub
    d_conv = zbuf.shape[2]
    d_pool = ubuf.shape[2]
    n_groups = len(POOL_WINDOWS)
    gc = d_pool // n_groups

    @pl.when((pl.program_id(0) == 0) & (j == 0))
    def _():
        wcat[0:d_conv, :] = wout_ref[0:d_conv, :]
        for g in range(n_groups):
            lo = g * gc
            p_g = (pw_ref[g] * ps_ref[:, lo:lo + gc]).astype(jnp.bfloat16)
            wcat[d_conv + lo:d_conv + lo + gc, :] = jnp.dot(
                p_g, wout_ref[d_conv + lo:d_conv + lo + gc, :],
                preferred_element_type=jnp.float32).astype(jnp.bfloat16)

    @pl.when(j == 0)
    def _():
        zbuf[0, 0:CONV_HALO, :] = jnp.zeros((CONV_HALO, d_conv), jnp.float32)
        ubuf[0, 0:POOL_HALO, :] = jnp.zeros((POOL_HALO, d_pool), jnp.float32)

    _side_cast(side_in, side_out)

    for k in range(n_sub):
        rows = pl.ds(k * t_sub, t_sub)
        nxt = (k + 1) % n_sub
        x = x_ref[0, rows, :]
        h = _rms_norm(x, g_ref[...]).astype(jnp.bfloat16)

        proj = jnp.dot(h, win_ref[...], preferred_element_type=jnp.float32)
        v = proj[:, 0:d_conv]
        gate_b = proj[:, d_conv:2 * d_conv]
        gate_c = proj[:, 2 * d_conv:3 * d_conv]
        ubuf[k, POOL_HALO:, :] = proj[:, 3 * d_conv:]
        zbuf[k, CONV_HALO:, :] = gate_c * v
        ze = zbuf[k]
        conv = cw_ref[:, (CONV_WIDTH - 1) * d_conv:] * ze
        for tap in range(1, CONV_WIDTH):
            conv = conv + (cw_ref[:, (CONV_WIDTH - 1 - tap) * d_conv:(CONV_WIDTH - tap) * d_conv]
                           * pltpu.roll(ze, tap, axis=0))
        y_a = gate_b * conv[CONV_HALO:, :]
        zbuf[nxt, 0:CONV_HALO, :] = ze[t_sub:, :]

        head = POOL_HALO
        head_pos = (j * t_tile + k * t_sub + 1
                    + lax.broadcasted_iota(jnp.int32, (head, gc), 0))
        y_b = []
        for g, w in enumerate(POOL_WINDOWS):
            ue = ubuf[k, :, g * gc:(g + 1) * gc]
            s = ue
            span = 1
            while span < w:
                s = s + pltpu.roll(s, span, axis=0)
                span *= 2
            s = s[POOL_HALO:, :]
            head_count = jnp.minimum(head_pos, w).astype(jnp.float32)
            mean = jnp.concatenate(
                [s[:head, :] / head_count, s[head:, :] * (1.0 / w)], axis=0)
            y_b.append(mean - ue[POOL_HALO:, :])
        ubuf[nxt, 0:POOL_HALO, :] = ubuf[k, t_sub:, :]
        y = jnp.dot(jnp.concatenate([y_a] + y_b, axis=-1).astype(jnp.bfloat16), wcat[...],
                    preferred_element_type=jnp.float32)
        o_ref[0, rows, :] = x + y


def _mixer(x, gain, w_in, conv_w, pool_w, pool_scale, w_out, side):
    b, s, d_model = x.shape
    d_conv = conv_w.shape[1] // CONV_WIDTH
    d_pool = pool_scale.shape[1]
    assert s % MIXER_TOKEN_TILE == 0 and MIXER_TOKEN_TILE % MIXER_SUB_TILES == 0
    t_sub = MIXER_TOKEN_TILE // MIXER_SUB_TILES
    assert t_sub % V7X_SUBLANES == 0 and t_sub >= POOL_HALO
    grid = (b, s // MIXER_TOKEN_TILE)
    tile = pl.BlockSpec((1, MIXER_TOKEN_TILE, d_model), lambda i, j: (i, j, 0))
    side_specs, side_shapes = _side_specs(side, grid)
    outs = pl.pallas_call(
        functools.partial(_mixer_kernel, n_side=len(side)),
        grid=grid,
        in_specs=[
            tile,
            _resident((1, d_model)),
            _resident(w_in.shape),
            _resident(conv_w.shape),
            _resident(pool_w.shape),
            _resident(pool_scale.shape),
            _resident(w_out.shape),
        ] + side_specs,
        out_specs=[tile] + side_specs,
        out_shape=[jax.ShapeDtypeStruct(x.shape, x.dtype)] + side_shapes,
        scratch_shapes=[
            pltpu.VMEM((MIXER_SUB_TILES, CONV_HALO + t_sub, d_conv), jnp.float32),
            pltpu.VMEM((MIXER_SUB_TILES, POOL_HALO + t_sub, d_pool), jnp.float32),
            pltpu.VMEM(w_out.shape, jnp.bfloat16),
        ],
        compiler_params=pltpu.CompilerParams(
            dimension_semantics=("arbitrary", "arbitrary"),
            vmem_limit_bytes=VMEM_LIMIT_BYTES),
        name="mixer",
    )(x, gain, w_in, conv_w, pool_w, pool_scale, w_out, *side)
    return outs[0], list(outs[1:])


def kernel(x, norm_ffn1, ffn1_w_gate, ffn1_w_up, ffn1_w_down, norm_mix, w_in, conv_w,
           pool_w, pool_scale, w_out, norm_ffn2, ffn2_w_gate, ffn2_w_up, ffn2_w_down,
           norm_final):
    b, s, d_model = x.shape
    depth = norm_ffn1.shape[0]
    assert depth >= 1
    final_gain = norm_final.reshape(1, d_model)
    ffn1_w = [ffn1_w_gate[0], ffn1_w_up[0], ffn1_w_down[0]]
    for l in range(depth):
        last = l == depth - 1
        mix_w = [w_in[l], w_out[l]]
        ffn2_w = [ffn2_w_gate[l], ffn2_w_up[l], ffn2_w_down[l]]
        next_w = [] if last else [ffn1_w_gate[l + 1], ffn1_w_up[l + 1], ffn1_w_down[l + 1]]

        x2d, mix_w = _ffn(x.reshape(b * s, d_model), norm_ffn1[l].reshape(1, d_model),
                          *ffn1_w, final_gain, mix_w, final_norm=False)
        x, ffn2_w = _mixer(x2d.reshape(b, s, d_model), norm_mix[l].reshape(1, d_model),
                           mix_w[0], conv_w[l].reshape(1, -1), pool_w[l],
                           pool_scale[l].reshape(1, -1),
                           mix_w[1], ffn2_w)
        x2d, ffn1_w = _ffn(x.reshape(b * s, d_model), norm_ffn2[l].reshape(1, d_model),
                           *ffn2_w, final_gain, next_w, final_norm=last)
        x = x2d.reshape(b, s, d_model)
    return x
```

```python
import functools

import jax
import jax.numpy as jnp
from jax import lax
from jax.experimental import pallas as pl
from jax.experimental.pallas import tpu as pltpu

EPS = 1e-6
CONV_WIDTH = 3
POOL_WINDOWS = (2, 4, 8, 16)

V7X_SUBLANES = 8
V7X_BF16_ROWS = 16
V7X_VMEM_BYTES = 64 * 1024 * 1024

FFN_TOKEN_TILE = 1024
FFN_SUB_TILES = 4
MIXER_TOKEN_TILE = 1024
MIXER_SUB_TILES = 2
CONV_HALO = V7X_SUBLANES
POOL_HALO = 2 * V7X_SUBLANES
LOAD_CHUNKS = 16
LOAD_DEPTH = 7
VMEM_LIMIT_BYTES = V7X_VMEM_BYTES * 7 // 8


def _rms_norm(x, gain):
    y = x * lax.rsqrt(jnp.mean(x * x, axis=-1, keepdims=True) + EPS)
    return y * gain


def _resident(shape):
    return pl.BlockSpec(shape, lambda *_: (0,) * len(shape), pipeline_mode=pl.Buffered(1))


def _side_specs(weights, grid):
    n_steps = 1
    for g in grid:
        n_steps *= g
    specs, shapes = [], []
    for w in weights:
        rows, cols = w.shape
        hold = 1
        while (rows * hold) % n_steps or (rows * hold // n_steps) % V7X_BF16_ROWS:
            hold *= 2
            assert hold <= n_steps, (rows, n_steps)

        def index_map(*idx, hold=hold):
            step = idx[0]
            for g, i in zip(grid[1:], idx[1:]):
                step = step * g + i
            return (step // hold, 0)

        specs.append(pl.BlockSpec((rows * hold // n_steps, cols), index_map))
        shapes.append(jax.ShapeDtypeStruct(w.shape, jnp.bfloat16))
    return specs, shapes


def _side_cast(src_refs, dst_refs):
    for src, dst in zip(src_refs, dst_refs):
        dst[...] = src[...].astype(jnp.bfloat16)


def _load_cast(jobs):
    chunks = []
    ring_pos = {}
    for w_hbm, w_vmem, stage, sem in jobs:
        rows = stage.shape[1]
        for c in range(w_hbm.shape[0] // rows):
            slot = ring_pos.get(id(stage), 0)
            ring_pos[id(stage)] = (slot + 1) % LOAD_DEPTH
            chunks.append((w_hbm, w_vmem, stage, sem, c * rows, rows, slot))

    def copy(k):
        w_hbm, _, stage, sem, row0, rows, slot = chunks[k]
        return pltpu.make_async_copy(w_hbm.at[pl.ds(row0, rows)], stage.at[slot], sem.at[slot])

    for k in range(min(LOAD_DEPTH, len(chunks))):
        copy(k).start()
    for k, (_, w_vmem, stage, _, row0, rows, slot) in enumerate(chunks):
        copy(k).wait()
        w_vmem[pl.ds(row0, rows), :] = stage[slot].astype(jnp.bfloat16)
        if k + LOAD_DEPTH < len(chunks):
            copy(k + LOAD_DEPTH).start()


def _ffn_kernel(*refs, n_side, final_norm, cast_own):
    x_ref, g_ref, wg_ref, wu_ref, wd_ref, gf_ref = refs[:6]
    side_in = refs[6:6 + n_side]
    o_ref = refs[6 + n_side]
    side_out = refs[7 + n_side:7 + 2 * n_side]
    scratch = refs[7 + 2 * n_side:]

    if cast_own:
        wg_bf, wu_bf, wd_bf, stage_in, stage_out, sem_in, sem_out = scratch

        @pl.when(pl.program_id(0) == 0)
        def _():
            _load_cast([(wg_ref, wg_bf, stage_in, sem_in),
                        (wu_ref, wu_bf, stage_in, sem_in),
                        (wd_ref, wd_bf, stage_out, sem_out)])

        wg_ref, wu_ref, wd_ref = wg_bf, wu_bf, wd_bf

    _side_cast(side_in, side_out)
    t_sub = x_ref.shape[0] // FFN_SUB_TILES
    acts = []
    for k in range(FFN_SUB_TILES):
        rows = pl.ds(k * t_sub, t_sub)
        h = _rms_norm(x_ref[rows, :], g_ref[...]).astype(jnp.bfloat16)
        gate = jnp.dot(h, wg_ref[...], preferred_element_type=jnp.float32)
        up = jnp.dot(h, wu_ref[...], preferred_element_type=jnp.float32)
        acts.append((gate * jax.nn.sigmoid(gate) * up).astype(jnp.bfloat16))
    for k in range(FFN_SUB_TILES):
        rows = pl.ds(k * t_sub, t_sub)
        y = jnp.dot(acts[k], wd_ref[...], preferred_element_type=jnp.float32)
        out = x_ref[rows, :] + 0.5 * y
        if final_norm:
            out = _rms_norm(out, gf_ref[...])
        o_ref[rows, :] = out


def _ffn(x2d, gain, w_gate, w_up, w_down, final_gain, side, *, final_norm):
    n_tok, d_model = x2d.shape
    d_ff = w_gate.shape[1]
    assert n_tok % FFN_TOKEN_TILE == 0
    grid = (n_tok // FFN_TOKEN_TILE,)
    cast_own = w_gate.dtype == jnp.float32
    assert w_up.dtype == w_gate.dtype and w_down.dtype == w_gate.dtype
    tile = pl.BlockSpec((FFN_TOKEN_TILE, d_model), lambda i: (i, 0))
    side_specs, side_shapes = _side_specs(side, grid)
    if cast_own:
        assert d_model % (LOAD_CHUNKS * V7X_BF16_ROWS) == 0
        assert d_ff % (LOAD_CHUNKS * V7X_BF16_ROWS) == 0
        hbm = pl.BlockSpec(memory_space=pl.ANY)
        w_specs = [hbm, hbm, hbm]
        scratch = [
            pltpu.VMEM((d_model, d_ff), jnp.bfloat16),
            pltpu.VMEM((d_model, d_ff), jnp.bfloat16),
            pltpu.VMEM((d_ff, d_model), jnp.bfloat16),
            pltpu.VMEM((LOAD_DEPTH, d_model // LOAD_CHUNKS, d_ff), jnp.float32),
            pltpu.VMEM((LOAD_DEPTH, d_ff // LOAD_CHUNKS, d_model), jnp.float32),
            pltpu.SemaphoreType.DMA((LOAD_DEPTH,)),
            pltpu.SemaphoreType.DMA((LOAD_DEPTH,)),
        ]
    else:
        w_specs = [_resident(w_gate.shape), _resident(w_up.shape), _resident(w_down.shape)]
        scratch = []
    outs = pl.pallas_call(
        functools.partial(_ffn_kernel, n_side=len(side), final_norm=final_norm,
                          cast_own=cast_own),
        grid=grid,
        in_specs=[tile, _resident((1, d_model))] + w_specs + [_resident((1, d_model))]
        + side_specs,
        out_specs=[tile] + side_specs,
        out_shape=[jax.ShapeDtypeStruct(x2d.shape, x2d.dtype)] + side_shapes,
        scratch_shapes=scratch,
        compiler_params=pltpu.CompilerParams(
            dimension_semantics=("arbitrary",), vmem_limit_bytes=VMEM_LIMIT_BYTES),
        name="ffn_final" if final_norm else "ffn",
    )(x2d, gain, w_gate, w_up, w_down, final_gain, *side)
    return outs[0], list(outs[1:])


def _mixer_kernel(*refs, n_side):
    x_ref, g_ref, win_ref, cw_ref, pw_ref, ps_ref, wout_ref = refs[:7]
    side_in = refs[7:7 + n_side]
    o_ref = refs[7 + n_side]
    side_out = refs[8 + n_side:8 + 2 * n_side]
    zbuf, ubuf, wcat = refs[8 + 2 * n_side:]
    j = pl.program_id(1)
    t_tile = x_ref.shape[1]
    n_sub = zbuf.shape[0]
    t_sub = t_tile // n_sub
    d_conv = zbuf.shape[2]
    d_pool = ubuf.shape[2]
    n_groups = len(POOL_WINDOWS)
    gc = d_pool // n_groups

    @pl.when((pl.program_id(0) == 0) & (j == 0))
    def _():
        wcat[0:d_conv, :] = wout_ref[0:d_conv, :]
        for g in range(n_groups):
            lo = g * gc
            p_g = (pw_ref[g] * ps_ref[:, lo:lo + gc]).astype(jnp.bfloat16)
            wcat[d_conv + lo:d_conv + lo + gc, :] = jnp.dot(
                p_g, wout_ref[d_conv + lo:d_conv + lo + gc, :],
                preferred_element_type=jnp.float32).astype(jnp.bfloat16)

    @pl.when(j == 0)
    def _():
        zbuf[0, 0:CONV_HALO, :] = jnp.zeros((CONV_HALO, d_conv), jnp.float32)
        ubuf[0, 0:POOL_HALO, :] = jnp.zeros((POOL_HALO, d_pool), jnp.float32)

    _side_cast(side_in, side_out)

    for k in range(n_sub):
        rows = pl.ds(k * t_sub, t_sub)
        nxt = (k + 1) % n_sub
        x = x_ref[0, rows, :]
        h = _rms_norm(x, g_ref[...]).astype(jnp.bfloat16)

        proj = jnp.dot(h, win_ref[...], preferred_element_type=jnp.float32)
        v = proj[:, 0:d_conv]
        gate_b = proj[:, d_conv:2 * d_conv]
        gate_c = proj[:, 2 * d_conv:3 * d_conv]
        ubuf[k, POOL_HALO:, :] = proj[:, 3 * d_conv:]
        zbuf[k, CONV_HALO:, :] = gate_c * v
        ze = zbuf[k]
        conv = cw_ref[:, (CONV_WIDTH - 1) * d_conv:] * ze
        for tap in range(1, CONV_WIDTH):
            conv = conv + (cw_ref[:, (CONV_WIDTH - 1 - tap) * d_conv:(CONV_WIDTH - tap) * d_conv]
                           * pltpu.roll(ze, tap, axis=0))
        y_a = gate_b * conv[CONV_HALO:, :]
        zbuf[nxt, 0:CONV_HALO, :] = ze[t_sub:, :]

        head = POOL_HALO
        head_pos = (j * t_tile + k * t_sub + 1
                    + lax.broadcasted_iota(jnp.int32, (head, gc), 0))
        y_b = []
        for g, w in enumerate(POOL_WINDOWS):
            ue = ubuf[k, :, g * gc:(g + 1) * gc]
            s = ue
            span = 1
            while span < w:
                s = s + pltpu.roll(s, span, axis=0)
                span *= 2
            s = s[POOL_HALO:, :]
            head_count = jnp.minimum(head_pos, w).astype(jnp.float32)
            mean = jnp.concatenate(
                [s[:head, :] / head_count, s[head:, :] * (1.0 / w)], axis=0)
            y_b.append(mean - ue[POOL_HALO:, :])
        ubuf[nxt, 0:POOL_HALO, :] = ubuf[k, t_sub:, :]
        y = jnp.dot(jnp.concatenate([y_a] + y_b, axis=-1).astype(jnp.bfloat16), wcat[...],
                    preferred_element_type=jnp.float32)
        o_ref[0, rows, :] = x + y


def _mixer(x, gain, w_in, conv_w, pool_w, pool_scale, w_out, side):
    b, s, d_model = x.shape
    d_conv = conv_w.shape[1] // CONV_WIDTH
    d_pool = pool_scale.shape[1]
    assert s % MIXER_TOKEN_TILE == 0 and MIXER_TOKEN_TILE % MIXER_SUB_TILES == 0
    t_sub = MIXER_TOKEN_TILE // MIXER_SUB_TILES
    assert t_sub % V7X_SUBLANES == 0 and t_sub >= POOL_HALO
    grid = (b, s // MIXER_TOKEN_TILE)
    tile = pl.BlockSpec((1, MIXER_TOKEN_TILE, d_model), lambda i, j: (i, j, 0))
    side_specs, side_shapes = _side_specs(side, grid)
    outs = pl.pallas_call(
        functools.partial(_mixer_kernel, n_side=len(side)),
        grid=grid,
        in_specs=[
            tile,
            _resident((1, d_model)),
            _resident(w_in.shape),
            _resident(conv_w.shape),
            _resident(pool_w.shape),
            _resident(pool_scale.shape),
            _resident(w_out.shape),
        ] + side_specs,
        out_specs=[tile] + side_specs,
        out_shape=[jax.ShapeDtypeStruct(x.shape, x.dtype)] + side_shapes,
        scratch_shapes=[
            pltpu.VMEM((MIXER_SUB_TILES, CONV_HALO + t_sub, d_conv), jnp.float32),
            pltpu.VMEM((MIXER_SUB_TILES, POOL_HALO + t_sub, d_pool), jnp.float32),
            pltpu.VMEM(w_out.shape, jnp.bfloat16),
        ],
        compiler_params=pltpu.CompilerParams(
            dimension_semantics=("arbitrary", "arbitrary"),
            vmem_limit_bytes=VMEM_LIMIT_BYTES),
        name="mixer",
    )(x, gain, w_in, conv_w, pool_w, pool_scale, w_out, *side)
    return outs[0], list(outs[1:])


def kernel(x, norm_ffn1, ffn1_w_gate, ffn1_w_up, ffn1_w_down, norm_mix, w_in, conv_w,
           pool_w, pool_scale, w_out, norm_ffn2, ffn2_w_gate, ffn2_w_up, ffn2_w_down,
           norm_final):
    b, s, d_model = x.shape
    depth = norm_ffn1.shape[0]
    assert depth >= 1
    final_gain = norm_final.reshape(1, d_model)
    ffn1_w = [ffn1_w_gate[0], ffn1_w_up[0], ffn1_w_down[0]]
    for l in range(depth):
        last = l == depth - 1
        mix_w = [w_in[l], w_out[l]]
        ffn2_w = [ffn2_w_gate[l], ffn2_w_up[l], ffn2_w_down[l]]
        next_w = [] if last else [ffn1_w_gate[l + 1], ffn1_w_up[l + 1], ffn1_w_down[l + 1]]

        x2d, mix_w = _ffn(x.reshape(b * s, d_model), norm_ffn1[l].reshape(1, d_model),
                          *ffn1_w, final_gain, mix_w, final_norm=False)
        x, ffn2_w = _mixer(x2d.reshape(b, s, d_model), norm_mix[l].reshape(1, d_model),
                           mix_w[0], conv_w[l].reshape(1, -1), pool_w[l],
                           pool_scale[l].reshape(1, -1),
                           mix_w[1], ffn2_w)
        x2d, ffn1_w = _ffn(x.reshape(b * s, d_model), norm_ffn2[l].reshape(1, d_model),
                           *ffn2_w, final_gain, next_w, final_norm=last)
        x = x2d.reshape(b, s, d_model)
    return x
```

```python
import functools

import jax
import jax.numpy as jnp
from jax import lax
from jax.experimental import pallas as pl
from jax.experimental.pallas import tpu as pltpu

EPS = 1e-6
CONV_WIDTH = 3
POOL_WINDOWS = (2, 4, 8, 16)

V7X_SUBLANES = 8
V7X_BF16_ROWS = 16
V7X_VMEM_BYTES = 64 * 1024 * 1024

FFN_TOKEN_TILE = 1024
FFN_SUB_TILES = 4
MIXER_TOKEN_TILE = 1024
MIXER_SUB_TILES = 2
CONV_HALO = V7X_SUBLANES
POOL_HALO = 2 * V7X_SUBLANES
LOAD_CHUNKS = 8
LOAD_DEPTH = 4
VMEM_LIMIT_BYTES = V7X_VMEM_BYTES * 7 // 8


def _rms_norm(x, gain):
    y = x * lax.rsqrt(jnp.mean(x * x, axis=-1, keepdims=True) + EPS)
    return y * gain


def _resident(shape):
    return pl.BlockSpec(shape, lambda *_: (0,) * len(shape), pipeline_mode=pl.Buffered(1))


def _side_specs(weights, grid):
    n_steps = 1
    for g in grid:
        n_steps *= g
    specs, shapes = [], []
    for w in weights:
        rows, cols = w.shape
        hold = 1
        while (rows * hold) % n_steps or (rows * hold // n_steps) % V7X_BF16_ROWS:
            hold *= 2
            assert hold <= n_steps, (rows, n_steps)

        def index_map(*idx, hold=hold):
            step = idx[0]
            for g, i in zip(grid[1:], idx[1:]):
                step = step * g + i
            return (step // hold, 0)

        specs.append(pl.BlockSpec((rows * hold // n_steps, cols), index_map))
        shapes.append(jax.ShapeDtypeStruct(w.shape, jnp.bfloat16))
    return specs, shapes


def _side_cast(src_refs, dst_refs):
    for src, dst in zip(src_refs, dst_refs):
        dst[...] = src[...].astype(jnp.bfloat16)


def _load_cast(jobs):
    chunks = []
    ring_pos = {}
    for w_hbm, w_vmem, stage, sem in jobs:
        rows = stage.shape[1]
        for c in range(w_hbm.shape[0] // rows):
            slot = ring_pos.get(id(stage), 0)
            ring_pos[id(stage)] = (slot + 1) % LOAD_DEPTH
            chunks.append((w_hbm, w_vmem, stage, sem, c * rows, rows, slot))

    def copy(k):
        w_hbm, _, stage, sem, row0, rows, slot = chunks[k]
        return pltpu.make_async_copy(w_hbm.at[pl.ds(row0, rows)], stage.at[slot], sem.at[slot])

    for k in range(min(LOAD_DEPTH, len(chunks))):
        copy(k).start()
    for k, (_, w_vmem, stage, _, row0, rows, slot) in enumerate(chunks):
        copy(k).wait()
        w_vmem[pl.ds(row0, rows), :] = stage[slot].astype(jnp.bfloat16)
        if k + LOAD_DEPTH < len(chunks):
            copy(k + LOAD_DEPTH).start()


def _ffn_kernel(*refs, n_side, final_norm, cast_own):
    x_ref, g_ref, wg_ref, wu_ref, wd_ref, gf_ref = refs[:6]
    side_in = refs[6:6 + n_side]
    o_ref = refs[6 + n_side]
    side_out = refs[7 + n_side:7 + 2 * n_side]
    scratch = refs[7 + 2 * n_side:]

    if cast_own:
        wg_bf, wu_bf, wd_bf, stage_in, stage_out, sem_in, sem_out = scratch

        @pl.when(pl.program_id(0) == 0)
        def _():
            _load_cast([(wg_ref, wg_bf, stage_in, sem_in),
                        (wu_ref, wu_bf, stage_in, sem_in),
                        (wd_ref, wd_bf, stage_out, sem_out)])

        wg_ref, wu_ref, wd_ref = wg_bf, wu_bf, wd_bf

    _side_cast(side_in, side_out)
    t_sub = x_ref.shape[0] // FFN_SUB_TILES
    acts = []
    for k in range(FFN_SUB_TILES):
        rows = pl.ds(k * t_sub, t_sub)
        h = _rms_norm(x_ref[rows, :], g_ref[...]).astype(jnp.bfloat16)
        gate = jnp.dot(h, wg_ref[...], preferred_element_type=jnp.float32)
        up = jnp.dot(h, wu_ref[...], preferred_element_type=jnp.float32)
        acts.append((gate * jax.nn.sigmoid(gate) * up).astype(jnp.bfloat16))
    for k in range(FFN_SUB_TILES):
        rows = pl.ds(k * t_sub, t_sub)
        y = jnp.dot(acts[k], wd_ref[...], preferred_element_type=jnp.float32)
        out = x_ref[rows, :] + 0.5 * y
        if final_norm:
            out = _rms_norm(out, gf_ref[...])
        o_ref[rows, :] = out


def _ffn(x2d, gain, w_gate, w_up, w_down, final_gain, side, *, final_norm):
    n_tok, d_model = x2d.shape
    d_ff = w_gate.shape[1]
    assert n_tok % FFN_TOKEN_TILE == 0
    grid = (n_tok // FFN_TOKEN_TILE,)
    cast_own = w_gate.dtype == jnp.float32
    assert w_up.dtype == w_gate.dtype and w_down.dtype == w_gate.dtype
    tile = pl.BlockSpec((FFN_TOKEN_TILE, d_model), lambda i: (i, 0))
    side_specs, side_shapes = _side_specs(side, grid)
    if cast_own:
        assert d_model % (LOAD_CHUNKS * V7X_BF16_ROWS) == 0
        assert d_ff % (LOAD_CHUNKS * V7X_BF16_ROWS) == 0
        hbm = pl.BlockSpec(memory_space=pl.ANY)
        w_specs = [hbm, hbm, hbm]
        scratch = [
            pltpu.VMEM((d_model, d_ff), jnp.bfloat16),
            pltpu.VMEM((d_model, d_ff), jnp.bfloat16),
            pltpu.VMEM((d_ff, d_model), jnp.bfloat16),
            pltpu.VMEM((LOAD_DEPTH, d_model // LOAD_CHUNKS, d_ff), jnp.float32),
            pltpu.VMEM((LOAD_DEPTH, d_ff // LOAD_CHUNKS, d_model), jnp.float32),
            pltpu.SemaphoreType.DMA((LOAD_DEPTH,)),
            pltpu.SemaphoreType.DMA((LOAD_DEPTH,)),
        ]
    else:
        w_specs = [_resident(w_gate.shape), _resident(w_up.shape), _resident(w_down.shape)]
        scratch = []
    outs = pl.pallas_call(
        functools.partial(_ffn_kernel, n_side=len(side), final_norm=final_norm,
                          cast_own=cast_own),
        grid=grid,
        in_specs=[tile, _resident((1, d_model))] + w_specs + [_resident((1, d_model))]
        + side_specs,
        out_specs=[tile] + side_specs,
        out_shape=[jax.ShapeDtypeStruct(x2d.shape, x2d.dtype)] + side_shapes,
        scratch_shapes=scratch,
        compiler_params=pltpu.CompilerParams(
            dimension_semantics=("arbitrary",), vmem_limit_bytes=VMEM_LIMIT_BYTES),
        name="ffn_final" if final_norm else "ffn",
    )(x2d, gain, w_gate, w_up, w_down, final_gain, *side)
    return outs[0], list(outs[1:])


def _mixer_kernel(*refs, n_side):
    x_ref, g_ref, win_ref, cw_ref, pw_ref, ps_ref, wout_ref = refs[:7]
    side_in = refs[7:7 + n_side]
    o_ref = refs[7 + n_side]
    side_out = refs[8 + n_side:8 + 2 * n_side]
    zbuf, ubuf, wcat = refs[8 + 2 * n_side:]
    j = pl.program_id(1)
    t_tile = x_ref.shape[1]
    n_sub = zbuf.shape[0]
    t_sub = t_tile // n_sub
    d_conv = zbuf.shape[2]
    d_pool = ubuf.shape[2]
    n_groups = len(POOL_WINDOWS)
    gc = d_pool // n_groups

    @pl.when((pl.program_id(0) == 0) & (j == 0))
    def _():
        wcat[0:d_conv, :] = wout_ref[0:d_conv, :]
        for g in range(n_groups):
            lo = g * gc
            p_g = (pw_ref[g] * ps_ref[:, lo:lo + gc]).astype(jnp.bfloat16)
            wcat[d_conv + lo:d_conv + lo + gc, :] = jnp.dot(
                p_g, wout_ref[d_conv + lo:d_conv + lo + gc, :],
                preferred_element_type=jnp.float32).astype(jnp.bfloat16)

    @pl.when(j == 0)
    def _():
        zbuf[0, 0:CONV_HALO, :] = jnp.zeros((CONV_HALO, d_conv), jnp.float32)
        ubuf[0, 0:POOL_HALO, :] = jnp.zeros((POOL_HALO, d_pool), jnp.float32)

    _side_cast(side_in, side_out)

    for k in range(n_sub):
        rows = pl.ds(k * t_sub, t_sub)
        nxt = (k + 1) % n_sub
        x = x_ref[0, rows, :]
        h = _rms_norm(x, g_ref[...]).astype(jnp.bfloat16)

        proj = jnp.dot(h, win_ref[...], preferred_element_type=jnp.float32)
        v = proj[:, 0:d_conv]
        gate_b = proj[:, d_conv:2 * d_conv]
        gate_c = proj[:, 2 * d_conv:3 * d_conv]
        ubuf[k, POOL_HALO:, :] = proj[:, 3 * d_conv:]
        zbuf[k, CONV_HALO:, :] = gate_c * v
        ze = zbuf[k]
        conv = cw_ref[:, (CONV_WIDTH - 1) * d_conv:] * ze
        for tap in range(1, CONV_WIDTH):
            conv = conv + (cw_ref[:, (CONV_WIDTH - 1 - tap) * d_conv:(CONV_WIDTH - tap) * d_conv]
                           * pltpu.roll(ze, tap, axis=0))
        y_a = gate_b * conv[CONV_HALO:, :]
        zbuf[nxt, 0:CONV_HALO, :] = ze[t_sub:, :]

        head = POOL_HALO
        head_pos = (j * t_tile + k * t_sub + 1
                    + lax.broadcasted_iota(jnp.int32, (head, gc), 0))
        y_b = []
        for g, w in enumerate(POOL_WINDOWS):
            ue = ubuf[k, :, g * gc:(g + 1) * gc]
            s = ue
            span = 1
            while span < w:
                s = s + pltpu.roll(s, span, axis=0)
                span *= 2
            s = s[POOL_HALO:, :]
            head_count = jnp.minimum(head_pos, w).astype(jnp.float32)
            mean = jnp.concatenate(
                [s[:head, :] / head_count, s[head:, :] * (1.0 / w)], axis=0)
            y_b.append(mean - ue[POOL_HALO:, :])
        ubuf[nxt, 0:POOL_HALO, :] = ubuf[k, t_sub:, :]
        y = jnp.dot(jnp.concatenate([y_a] + y_b, axis=-1).astype(jnp.bfloat16), wcat[...],
                    preferred_element_type=jnp.float32)
        o_ref[0, rows, :] = x + y


def _mixer(x, gain, w_in, conv_w, pool_w, pool_scale, w_out, side):
    b, s, d_model = x.shape
    d_conv = conv_w.shape[1] // CONV_WIDTH
    d_pool = pool_scale.shape[1]
    assert s % MIXER_TOKEN_TILE == 0 and MIXER_TOKEN_TILE % MIXER_SUB_TILES == 0
    t_sub = MIXER_TOKEN_TILE // MIXER_SUB_TILES
    assert t_sub % V7X_SUBLANES == 0 and t_sub >= POOL_HALO
    grid = (b, s // MIXER_TOKEN_TILE)
    tile = pl.BlockSpec((1, MIXER_TOKEN_TILE, d_model), lambda i, j: (i, j, 0))
    side_specs, side_shapes = _side_specs(side, grid)
    outs = pl.pallas_call(
        functools.partial(_mixer_kernel, n_side=len(side)),
        grid=grid,
        in_specs=[
            tile,
            _resident((1, d_model)),
            _resident(w_in.shape),
            _resident(conv_w.shape),
            _resident(pool_w.shape),
            _resident(pool_scale.shape),
            _resident(w_out.shape),
        ] + side_specs,
        out_specs=[tile] + side_specs,
        out_shape=[jax.ShapeDtypeStruct(x.shape, x.dtype)] + side_shapes,
        scratch_shapes=[
            pltpu.VMEM((MIXER_SUB_TILES, CONV_HALO + t_sub, d_conv), jnp.float32),
            pltpu.VMEM((MIXER_SUB_TILES, POOL_HALO + t_sub, d_pool), jnp.float32),
            pltpu.VMEM(w_out.shape, jnp.bfloat16),
        ],
        compiler_params=pltpu.CompilerParams(
            dimension_semantics=("arbitrary", "arbitrary"),
            vmem_limit_bytes=VMEM_LIMIT_BYTES),
        name="mixer",
    )(x, gain, w_in, conv_w, pool_w, pool_scale, w_out, *side)
    return outs[0], list(outs[1:])


def kernel(x, norm_ffn1, ffn1_w_gate, ffn1_w_up, ffn1_w_down, norm_mix, w_in, conv_w,
           pool_w, pool_scale, w_out, norm_ffn2, ffn2_w_gate, ffn2_w_up, ffn2_w_down,
           norm_final):
    b, s, d_model = x.shape
    depth = norm_ffn1.shape[0]
    assert depth >= 1
    final_gain = norm_final.reshape(1, d_model)
    ffn1_w = [ffn1_w_gate[0], ffn1_w_up[0], ffn1_w_down[0]]
    for l in range(depth):
        last = l == depth - 1
        mix_w = [w_in[l], w_out[l]]
        ffn2_w = [ffn2_w_gate[l], ffn2_w_up[l], ffn2_w_down[l]]
        next_w = [] if last else [ffn1_w_gate[l + 1], ffn1_w_up[l + 1], ffn1_w_down[l + 1]]

        x2d, mix_w = _ffn(x.reshape(b * s, d_model), norm_ffn1[l].reshape(1, d_model),
                          *ffn1_w, final_gain, mix_w, final_norm=False)
        x, ffn2_w = _mixer(x2d.reshape(b, s, d_model), norm_mix[l].reshape(1, d_model),
                           mix_w[0], conv_w[l].reshape(1, -1), pool_w[l],
                           pool_scale[l].reshape(1, -1),
                           mix_w[1], ffn2_w)
        x2d, ffn1_w = _ffn(x.reshape(b * s, d_model), norm_ffn2[l].reshape(1, d_model),
                           *ffn2_w, final_gain, next_w, final_norm=last)
        x = x2d.reshape(b, s, d_model)
    return x
```

```python
import functools

import jax
import jax.numpy as jnp
from jax import lax
from jax.experimental import pallas as pl
from jax.experimental.pallas import tpu as pltpu

EPS = 1e-6
CONV_WIDTH = 3
POOL_WINDOWS = (2, 4, 8, 16)

V7X_SUBLANES = 8
V7X_BF16_ROWS = 16
V7X_VMEM_BYTES = 64 * 1024 * 1024

FFN_TOKEN_TILE = 1024
FFN_WIDE_TOKEN_TILE = 2048
FFN_BLOCK_ROWS = 256
FFN_OUT_LAG = 2
MIXER_TOKEN_TILE = 1024
MIXER_SUB_TILES = 2
CONV_HALO = V7X_SUBLANES
POOL_HALO = 2 * V7X_SUBLANES
LOAD_CHUNKS = 16
LOAD_DEPTH = 7
VMEM_LIMIT_BYTES = V7X_VMEM_BYTES * 15 // 16


def _rms_norm(x, gain):
    y = x * lax.rsqrt(jnp.mean(x * x, axis=-1, keepdims=True) + EPS)
    return y * gain


def _resident(shape):
    return pl.BlockSpec(shape, lambda *_: (0,) * len(shape), pipeline_mode=pl.Buffered(1))


def _side_specs(weights, grid):
    n_steps = 1
    for g in grid:
        n_steps *= g
    specs, shapes = [], []
    for w in weights:
        rows, cols = w.shape
        hold = 1
        while (rows * hold) % n_steps or (rows * hold // n_steps) % V7X_BF16_ROWS:
            hold *= 2
            assert hold <= n_steps, (rows, n_steps)

        def index_map(*idx, hold=hold):
            step = idx[0]
            for g, i in zip(grid[1:], idx[1:]):
                step = step * g + i
            return (step // hold, 0)

        specs.append(pl.BlockSpec((rows * hold // n_steps, cols), index_map))
        shapes.append(jax.ShapeDtypeStruct(w.shape, jnp.bfloat16))
    return specs, shapes


def _side_cast(src_refs, dst_refs):
    for src, dst in zip(src_refs, dst_refs):
        dst[...] = src[...].astype(jnp.bfloat16)


def _load_cast(jobs):
    chunks = []
    ring_pos = {}
    for w_hbm, w_vmem, stage, sem in jobs:
        rows = stage.shape[1]
        for c in range(w_hbm.shape[0] // rows):
            slot = ring_pos.get(id(stage), 0)
            ring_pos[id(stage)] = (slot + 1) % LOAD_DEPTH
            chunks.append((w_hbm, w_vmem, stage, sem, c * rows, rows, slot))

    def copy(k):
        w_hbm, _, stage, sem, row0, rows, slot = chunks[k]
        return pltpu.make_async_copy(w_hbm.at[pl.ds(row0, rows)], stage.at[slot], sem.at[slot])

    for k in range(min(LOAD_DEPTH, len(chunks))):
        copy(k).start()
    for k, (_, w_vmem, stage, _, row0, rows, slot) in enumerate(chunks):
        copy(k).wait()
        w_vmem[pl.ds(row0, rows), :] = stage[slot].astype(jnp.bfloat16)
        if k + LOAD_DEPTH < len(chunks):
            copy(k + LOAD_DEPTH).start()


def _ffn_kernel(*refs, n_side, final_norm, cast_own):
    x_ref, g_ref, wg_ref, wu_ref, wd_ref, gf_ref = refs[:6]
    side_in = refs[6:6 + n_side]
    o_ref = refs[6 + n_side]
    side_out = refs[7 + n_side:7 + 2 * n_side]
    scratch = refs[7 + 2 * n_side:]
    step = pl.program_id(0)
    t_tile = x_ref.shape[0]
    n_blocks = t_tile // FFN_BLOCK_ROWS

    if cast_own:
        wg_bf, wu_bf, wd_bf, stage_in, stage_out, sem_in, sem_out = scratch

        @pl.when(step == 0)
        def _():
            _load_cast([(wg_ref, wg_bf, stage_in, sem_in),
                        (wu_ref, wu_bf, stage_in, sem_in),
                        (wd_ref, wd_bf, stage_out, sem_out)])

        wg_ref, wu_ref, wd_ref = wg_bf, wu_bf, wd_bf
    else:
        out_ring, out_sem = scratch
        n_early = n_blocks - FFN_OUT_LAG
        late_base = n_early + FFN_OUT_LAG * (step % 2)

        def slot_of(k):
            return k if k < n_early else late_base + (k - n_early)

        def out_copy(slot, row):
            row = pl.multiple_of(row, FFN_BLOCK_ROWS)
            return pltpu.make_async_copy(out_ring.at[slot],
                                         o_ref.at[pl.ds(row, FFN_BLOCK_ROWS)], out_sem.at[slot])

        @pl.when(step > 0)
        def _():
            for k in range(n_early):
                out_copy(k, 0).wait()

        @pl.when(step > 1)
        def _():
            for k in range(n_early, n_blocks):
                out_copy(slot_of(k), 0).wait()

    _side_cast(side_in, side_out)
    t_sub = FFN_BLOCK_ROWS
    acts = []
    for k in range(n_blocks):
        rows = pl.ds(k * t_sub, t_sub)
        h = _rms_norm(x_ref[rows, :], g_ref[...]).astype(jnp.bfloat16)
        gate = jnp.dot(h, wg_ref[...], preferred_element_type=jnp.float32)
        up = jnp.dot(h, wu_ref[...], preferred_element_type=jnp.float32)
        acts.append((gate * jax.nn.sigmoid(gate) * up).astype(jnp.bfloat16))
    for k in range(n_blocks):
        rows = pl.ds(k * t_sub, t_sub)
        if cast_own:
            y = jnp.dot(acts[k], wd_ref[...], preferred_element_type=jnp.float32)
        else:
            half = wd_ref.shape[1] // 2
            y_lo = jnp.dot(acts[k], wd_ref[:, :half], preferred_element_type=jnp.float32)
            if k > 0:
                out_copy(slot_of(k - 1), step * t_tile + (k - 1) * t_sub).start()
            y_hi = jnp.dot(acts[k], wd_ref[:, half:], preferred_element_type=jnp.float32)
            y = jnp.concatenate([y_lo, y_hi], axis=-1)
        out = x_ref[rows, :] + 0.5 * y
        if final_norm:
            out = _rms_norm(out, gf_ref[...])
        if cast_own:
            o_ref[rows, :] = out
        else:
            out_ring[slot_of(k)] = out

    if not cast_own:
        out_copy(slot_of(n_blocks - 1), step * t_tile + (n_blocks - 1) * t_sub).start()

        @pl.when(step == pl.num_programs(0) - 1)
        def _():
            for slot in range(n_blocks + FFN_OUT_LAG):
                out_copy(slot, 0).wait()


def _ffn(x2d, gain, w_gate, w_up, w_down, final_gain, side, *, final_norm):
    n_tok, d_model = x2d.shape
    d_ff = w_gate.shape[1]
    cast_own = w_gate.dtype == jnp.float32
    assert w_up.dtype == w_gate.dtype and w_down.dtype == w_gate.dtype
    t_tile = FFN_TOKEN_TILE if cast_own else FFN_WIDE_TOKEN_TILE
    assert n_tok % t_tile == 0 and t_tile % FFN_BLOCK_ROWS == 0
    assert t_tile // FFN_BLOCK_ROWS > FFN_OUT_LAG and (cast_own or n_tok // t_tile >= 2)
    grid = (n_tok // t_tile,)
    tile = pl.BlockSpec((t_tile, d_model), lambda i: (i, 0))
    hbm = pl.BlockSpec(memory_space=pl.ANY)
    side_specs, side_shapes = _side_specs(side, grid)
    if cast_own:
        assert d_model % (LOAD_CHUNKS * V7X_BF16_ROWS) == 0
        assert d_ff % (LOAD_CHUNKS * V7X_BF16_ROWS) == 0
        w_specs = [hbm, hbm, hbm]
        scratch = [
            pltpu.VMEM((d_model, d_ff), jnp.bfloat16),
            pltpu.VMEM((d_model, d_ff), jnp.bfloat16),
            pltpu.VMEM((d_ff, d_model), jnp.bfloat16),
            pltpu.VMEM((LOAD_DEPTH, d_model // LOAD_CHUNKS, d_ff), jnp.float32),
            pltpu.VMEM((LOAD_DEPTH, d_ff // LOAD_CHUNKS, d_model), jnp.float32),
            pltpu.SemaphoreType.DMA((LOAD_DEPTH,)),
            pltpu.SemaphoreType.DMA((LOAD_DEPTH,)),
        ]
    else:
        w_specs = [_resident(w_gate.shape), _resident(w_up.shape), _resident(w_down.shape)]
        scratch = [
            pltpu.VMEM((t_tile // FFN_BLOCK_ROWS + FFN_OUT_LAG, FFN_BLOCK_ROWS, d_model),
                       jnp.float32),
            pltpu.SemaphoreType.DMA((t_tile // FFN_BLOCK_ROWS + FFN_OUT_LAG,)),
        ]
    outs = pl.pallas_call(
        functools.partial(_ffn_kernel, n_side=len(side), final_norm=final_norm,
                          cast_own=cast_own),
        grid=grid,
        in_specs=[tile, _resident((1, d_model))] + w_specs + [_resident((1, d_model))]
        + side_specs,
        out_specs=[tile if cast_own else hbm] + side_specs,
        out_shape=[jax.ShapeDtypeStruct(x2d.shape, x2d.dtype)] + side_shapes,
        scratch_shapes=scratch,
        compiler_params=pltpu.CompilerParams(
            dimension_semantics=("arbitrary",), vmem_limit_bytes=VMEM_LIMIT_BYTES),
        name="ffn_final" if final_norm else "ffn",
    )(x2d, gain, w_gate, w_up, w_down, final_gain, *side)
    return outs[0], list(outs[1:])


def _mixer_kernel(*refs, n_side):
    x_ref, g_ref, win_ref, cw_ref, pw_ref, ps_ref, wout_ref = refs[:7]
    side_in = refs[7:7 + n_side]
    o_ref = refs[7 + n_side]
    side_out = refs[8 + n_side:8 + 2 * n_side]
    zbuf, ubuf, wcat = refs[8 + 2 * n_side:]
    j = pl.program_id(1)
    t_tile = x_ref.shape[1]
    n_sub = zbuf.shape[0]
    t_sub = t_tile // n_sub
    d_conv = zbuf.shape[2]
    d_pool = ubuf.shape[2]
    n_groups = len(POOL_WINDOWS)
    gc = d_pool // n_groups

    @pl.when((pl.program_id(0) == 0) & (j == 0))
    def _():
        wcat[0:d_conv, :] = wout_ref[0:d_conv, :]
        for g in range(n_groups):
            lo = g * gc
            p_g = (pw_ref[g] * ps_ref[:, lo:lo + gc]).astype(jnp.bfloat16)
            wcat[d_conv + lo:d_conv + lo + gc, :] = jnp.dot(
                p_g, wout_ref[d_conv + lo:d_conv + lo + gc, :],
                preferred_element_type=jnp.float32).astype(jnp.bfloat16)

    @pl.when(j == 0)
    def _():
        zbuf[0, 0:CONV_HALO, :] = jnp.zeros((CONV_HALO, d_conv), jnp.float32)
        ubuf[0, 0:POOL_HALO, :] = jnp.zeros((POOL_HALO, d_pool), jnp.float32)

    _side_cast(side_in, side_out)

    for k in range(n_sub):
        rows = pl.ds(k * t_sub, t_sub)
        nxt = (k + 1) % n_sub
        x = x_ref[0, rows, :]
        h = _rms_norm(x, g_ref[...]).astype(jnp.bfloat16)

        proj = jnp.dot(h, win_ref[...], preferred_element_type=jnp.float32)
        v = proj[:, 0:d_conv]
        gate_b = proj[:, d_conv:2 * d_conv]
        gate_c = proj[:, 2 * d_conv:3 * d_conv]
        ubuf[k, POOL_HALO:, :] = proj[:, 3 * d_conv:]
        zbuf[k, CONV_HALO:, :] = gate_c * v
        ze = zbuf[k]
        conv = cw_ref[:, (CONV_WIDTH - 1) * d_conv:] * ze
        for tap in range(1, CONV_WIDTH):
            conv = conv + (cw_ref[:, (CONV_WIDTH - 1 - tap) * d_conv:(CONV_WIDTH - tap) * d_conv]
                           * pltpu.roll(ze, tap, axis=0))
        y_a = gate_b * conv[CONV_HALO:, :]
        zbuf[nxt, 0:CONV_HALO, :] = ze[t_sub:, :]

        head = POOL_HALO
        head_pos = (j * t_tile + k * t_sub + 1
                    + lax.broadcasted_iota(jnp.int32, (head, gc), 0))
        y_b = []
        for g, w in enumerate(POOL_WINDOWS):
            ue = ubuf[k, :, g * gc:(g + 1) * gc]
            s = ue
            span = 1
            while span < w:
                s = s + pltpu.roll(s, span, axis=0)
                span *= 2
            s = s[POOL_HALO:, :]
            head_count = jnp.minimum(head_pos, w).astype(jnp.float32)
            mean = jnp.concatenate(
                [s[:head, :] / head_count, s[head:, :] * (1.0 / w)], axis=0)
            y_b.append(mean - ue[POOL_HALO:, :])
        ubuf[nxt, 0:POOL_HALO, :] = ubuf[k, t_sub:, :]
        y = jnp.dot(jnp.concatenate([y_a] + y_b, axis=-1).astype(jnp.bfloat16), wcat[...],
                    preferred_element_type=jnp.float32)
        o_ref[0, rows, :] = x + y


def _mixer(x, gain, w_in, conv_w, pool_w, pool_scale, w_out, side):
    b, s, d_model = x.shape
    d_conv = conv_w.shape[1] // CONV_WIDTH
    d_pool = pool_scale.shape[1]
    assert s % MIXER_TOKEN_TILE == 0 and MIXER_TOKEN_TILE % MIXER_SUB_TILES == 0
    t_sub = MIXER_TOKEN_TILE // MIXER_SUB_TILES
    assert t_sub % V7X_SUBLANES == 0 and t_sub >= POOL_HALO
    grid = (b, s // MIXER_TOKEN_TILE)
    tile = pl.BlockSpec((1, MIXER_TOKEN_TILE, d_model), lambda i, j: (i, j, 0))
    side_specs, side_shapes = _side_specs(side, grid)
    outs = pl.pallas_call(
        functools.partial(_mixer_kernel, n_side=len(side)),
        grid=grid,
        in_specs=[
            tile,
            _resident((1, d_model)),
            _resident(w_in.shape),
            _resident(conv_w.shape),
            _resident(pool_w.shape),
            _resident(pool_scale.shape),
            _resident(w_out.shape),
        ] + side_specs,
        out_specs=[tile] + side_specs,
        out_shape=[jax.ShapeDtypeStruct(x.shape, x.dtype)] + side_shapes,
        scratch_shapes=[
            pltpu.VMEM((MIXER_SUB_TILES, CONV_HALO + t_sub, d_conv), jnp.float32),
            pltpu.VMEM((MIXER_SUB_TILES, POOL_HALO + t_sub, d_pool), jnp.float32),
            pltpu.VMEM(w_out.shape, jnp.bfloat16),
        ],
        compiler_params=pltpu.CompilerParams(
            dimension_semantics=("arbitrary", "arbitrary"),
            vmem_limit_bytes=VMEM_LIMIT_BYTES),
        name="mixer",
    )(x, gain, w_in, conv_w, pool_w, pool_scale, w_out, *side)
    return outs[0], list(outs[1:])


def kernel(x, norm_ffn1, ffn1_w_gate, ffn1_w_up, ffn1_w_down, norm_mix, w_in, conv_w,
           pool_w, pool_scale, w_out, norm_ffn2, ffn2_w_gate, ffn2_w_up, ffn2_w_down,
           norm_final):
    b, s, d_model = x.shape
    depth = norm_ffn1.shape[0]
    assert depth >= 1
    final_gain = norm_final.reshape(1, d_model)
    ffn1_w = [ffn1_w_gate[0], ffn1_w_up[0], ffn1_w_down[0]]
    for l in range(depth):
        last = l == depth - 1
        mix_w = [w_in[l], w_out[l]]
        ffn2_w = [ffn2_w_gate[l], ffn2_w_up[l], ffn2_w_down[l]]
        next_w = [] if last else [ffn1_w_gate[l + 1], ffn1_w_up[l + 1], ffn1_w_down[l + 1]]

        x2d, mix_w = _ffn(x.reshape(b * s, d_model), norm_ffn1[l].reshape(1, d_model),
                          *ffn1_w, final_gain, mix_w, final_norm=False)
        x, ffn2_w = _mixer(x2d.reshape(b, s, d_model), norm_mix[l].reshape(1, d_model),
                           mix_w[0], conv_w[l].reshape(1, -1), pool_w[l],
                           pool_scale[l].reshape(1, -1),
                           mix_w[1], ffn2_w)
        x2d, ffn1_w = _ffn(x.reshape(b * s, d_model), norm_ffn2[l].reshape(1, d_model),
                           *ffn2_w, final_gain, next_w, final_norm=last)
        x = x2d.reshape(b, s, d_model)
    return x
```

```python
import functools

import jax
import jax.numpy as jnp
from jax import lax
from jax.experimental import pallas as pl
from jax.experimental.pallas import tpu as pltpu

EPS = 1e-6
CONV_WIDTH = 3
POOL_WINDOWS = (2, 4, 8, 16)

V7X_SUBLANES = 8
V7X_BF16_ROWS = 16
V7X_VMEM_BYTES = 64 * 1024 * 1024

FFN_TOKEN_TILE = 1024
FFN_SUB_TILES = 4
MIXER_TOKEN_TILE = 1024
MIXER_SUB_TILES = 2
CONV_HALO = V7X_SUBLANES
POOL_HALO = 2 * V7X_SUBLANES
LOAD_CHUNKS = 16
LOAD_DEPTH = 7
VMEM_LIMIT_BYTES = V7X_VMEM_BYTES * 15 // 16


def _rms_norm(x, gain):
    y = x * lax.rsqrt(jnp.mean(x * x, axis=-1, keepdims=True) + EPS)
    return y * gain


def _resident(shape):
    return pl.BlockSpec(shape, lambda *_: (0,) * len(shape), pipeline_mode=pl.Buffered(1))


def _side_specs(weights, grid):
    n_steps = 1
    for g in grid:
        n_steps *= g
    specs, shapes = [], []
    for w in weights:
        rows, cols = w.shape
        hold = 1
        while (rows * hold) % n_steps or (rows * hold // n_steps) % V7X_BF16_ROWS:
            hold *= 2
            assert hold <= n_steps, (rows, n_steps)

        def index_map(*idx, hold=hold):
            step = idx[0]
            for g, i in zip(grid[1:], idx[1:]):
                step = step * g + i
            return (step // hold, 0)

        specs.append(pl.BlockSpec((rows * hold // n_steps, cols), index_map))
        shapes.append(jax.ShapeDtypeStruct(w.shape, jnp.bfloat16))
    return specs, shapes


def _side_cast(src_refs, dst_refs):
    for src, dst in zip(src_refs, dst_refs):
        dst[...] = src[...].astype(jnp.bfloat16)


def _load_cast(jobs):
    chunks = []
    ring_pos = {}
    for w_hbm, w_vmem, stage, sem in jobs:
        rows = stage.shape[1]
        for c in range(w_hbm.shape[0] // rows):
            slot = ring_pos.get(id(stage), 0)
            ring_pos[id(stage)] = (slot + 1) % LOAD_DEPTH
            chunks.append((w_hbm, w_vmem, stage, sem, c * rows, rows, slot))

    def copy(k):
        w_hbm, _, stage, sem, row0, rows, slot = chunks[k]
        return pltpu.make_async_copy(w_hbm.at[pl.ds(row0, rows)], stage.at[slot], sem.at[slot])

    for k in range(min(LOAD_DEPTH, len(chunks))):
        copy(k).start()
    for k, (_, w_vmem, stage, _, row0, rows, slot) in enumerate(chunks):
        copy(k).wait()
        w_vmem[pl.ds(row0, rows), :] = stage[slot].astype(jnp.bfloat16)
        if k + LOAD_DEPTH < len(chunks):
            copy(k + LOAD_DEPTH).start()


def _ffn_kernel(*refs, n_side, final_norm, cast_own):
    x_ref, g_ref, wg_ref, wu_ref, wd_ref, gf_ref = refs[:6]
    side_in = refs[6:6 + n_side]
    o_ref = refs[6 + n_side]
    side_out = refs[7 + n_side:7 + 2 * n_side]
    scratch = refs[7 + 2 * n_side:]

    if cast_own:
        wg_bf, wu_bf, wd_bf, stage_in, stage_out, sem_in, sem_out = scratch

        @pl.when(pl.program_id(0) == 0)
        def _():
            _load_cast([(wg_ref, wg_bf, stage_in, sem_in),
                        (wu_ref, wu_bf, stage_in, sem_in),
                        (wd_ref, wd_bf, stage_out, sem_out)])

        wg_ref, wu_ref, wd_ref = wg_bf, wu_bf, wd_bf

    _side_cast(side_in, side_out)
    t_sub = x_ref.shape[0] // FFN_SUB_TILES
    acts = []
    for k in range(FFN_SUB_TILES):
        rows = pl.ds(k * t_sub, t_sub)
        h = _rms_norm(x_ref[rows, :], g_ref[...]).astype(jnp.bfloat16)
        gate = jnp.dot(h, wg_ref[...], preferred_element_type=jnp.float32)
        up = jnp.dot(h, wu_ref[...], preferred_element_type=jnp.float32)
        acts.append((gate * jax.nn.sigmoid(gate) * up).astype(jnp.bfloat16))
    for k in range(FFN_SUB_TILES):
        rows = pl.ds(k * t_sub, t_sub)
        y = jnp.dot(acts[k], wd_ref[...], preferred_element_type=jnp.float32)
        out = x_ref[rows, :] + 0.5 * y
        if final_norm:
            out = _rms_norm(out, gf_ref[...])
        o_ref[rows, :] = out


def _ffn(x2d, gain, w_gate, w_up, w_down, final_gain, side, *, final_norm):
    n_tok, d_model = x2d.shape
    d_ff = w_gate.shape[1]
    assert n_tok % FFN_TOKEN_TILE == 0
    grid = (n_tok // FFN_TOKEN_TILE,)
    cast_own = w_gate.dtype == jnp.float32
    assert w_up.dtype == w_gate.dtype and w_down.dtype == w_gate.dtype
    tile = pl.BlockSpec((FFN_TOKEN_TILE, d_model), lambda i: (i, 0))
    side_specs, side_shapes = _side_specs(side, grid)
    if cast_own:
        assert d_model % (LOAD_CHUNKS * V7X_BF16_ROWS) == 0
        assert d_ff % (LOAD_CHUNKS * V7X_BF16_ROWS) == 0
        hbm = pl.BlockSpec(memory_space=pl.ANY)
        w_specs = [hbm, hbm, hbm]
        scratch = [
            pltpu.VMEM((d_model, d_ff), jnp.bfloat16),
            pltpu.VMEM((d_model, d_ff), jnp.bfloat16),
            pltpu.VMEM((d_ff, d_model), jnp.bfloat16),
            pltpu.VMEM((LOAD_DEPTH, d_model // LOAD_CHUNKS, d_ff), jnp.float32),
            pltpu.VMEM((LOAD_DEPTH, d_ff // LOAD_CHUNKS, d_model), jnp.float32),
            pltpu.SemaphoreType.DMA((LOAD_DEPTH,)),
            pltpu.SemaphoreType.DMA((LOAD_DEPTH,)),
        ]
    else:
        w_specs = [_resident(w_gate.shape), _resident(w_up.shape), _resident(w_down.shape)]
        scratch = []
    outs = pl.pallas_call(
        functools.partial(_ffn_kernel, n_side=len(side), final_norm=final_norm,
                          cast_own=cast_own),
        grid=grid,
        in_specs=[tile, _resident((1, d_model))] + w_specs + [_resident((1, d_model))]
        + side_specs,
        out_specs=[tile] + side_specs,
        out_shape=[jax.ShapeDtypeStruct(x2d.shape, x2d.dtype)] + side_shapes,
        scratch_shapes=scratch,
        compiler_params=pltpu.CompilerParams(
            dimension_semantics=("arbitrary",), vmem_limit_bytes=VMEM_LIMIT_BYTES),
        name="ffn_final" if final_norm else "ffn",
    )(x2d, gain, w_gate, w_up, w_down, final_gain, *side)
    return outs[0], list(outs[1:])


def _mixer_kernel(*refs, n_side):
    x_ref, g_ref, win_ref, cw_ref, pw_ref, ps_ref, wout_ref = refs[:7]
    side_in = refs[7:7 + n_side]
    o_ref = refs[7 + n_side]
    side_out = refs[8 + n_side:8 + 2 * n_side]
    zbuf, ubuf, wcat = refs[8 + 2 * n_side:]
    j = pl.program_id(1)
    t_tile = x_ref.shape[1]
    n_sub = zbuf.shape[0]
    t_sub = t_tile // n_sub
    d_conv = zbuf.shape[2]
    d_pool = ubuf.shape[2]
    n_groups = len(POOL_WINDOWS)
    gc = d_pool // n_groups

    @pl.when((pl.program_id(0) == 0) & (j == 0))
    def _():
        wcat[0:d_conv, :] = wout_ref[0:d_conv, :]
        for g in range(n_groups):
            lo = g * gc
            p_g = (pw_ref[g] * ps_ref[:, lo:lo + gc]).astype(jnp.bfloat16)
            wcat[d_conv + lo:d_conv + lo + gc, :] = jnp.dot(
                p_g, wout_ref[d_conv + lo:d_conv + lo + gc, :],
                preferred_element_type=jnp.float32).astype(jnp.bfloat16)

    @pl.when(j == 0)
    def _():
        zbuf[0, 0:CONV_HALO, :] = jnp.zeros((CONV_HALO, d_conv), jnp.float32)
        ubuf[0, 0:POOL_HALO, :] = jnp.zeros((POOL_HALO, d_pool), jnp.float32)

    _side_cast(side_in, side_out)

    for k in range(n_sub):
        rows = pl.ds(k * t_sub, t_sub)
        nxt = (k + 1) % n_sub
        x = x_ref[0, rows, :]
        h = _rms_norm(x, g_ref[...]).astype(jnp.bfloat16)

        proj = jnp.dot(h, win_ref[...], preferred_element_type=jnp.float32)
        v = proj[:, 0:d_conv]
        gate_b = proj[:, d_conv:2 * d_conv]
        gate_c = proj[:, 2 * d_conv:3 * d_conv]
        ubuf[k, POOL_HALO:, :] = proj[:, 3 * d_conv:]
        zbuf[k, CONV_HALO:, :] = gate_c * v
        ze = zbuf[k]
        conv = cw_ref[:, (CONV_WIDTH - 1) * d_conv:] * ze
        for tap in range(1, CONV_WIDTH):
            conv = conv + (cw_ref[:, (CONV_WIDTH - 1 - tap) * d_conv:(CONV_WIDTH - tap) * d_conv]
                           * pltpu.roll(ze, tap, axis=0))
        y_a = gate_b * conv[CONV_HALO:, :]
        zbuf[nxt, 0:CONV_HALO, :] = ze[t_sub:, :]

        head = POOL_HALO
        head_pos = (j * t_tile + k * t_sub + 1
                    + lax.broadcasted_iota(jnp.int32, (head, gc), 0))
        y_b = []
        for g, w in enumerate(POOL_WINDOWS):
            ue = ubuf[k, :, g * gc:(g + 1) * gc]
            s = ue
            span = 1
            while span < w:
                s = s + pltpu.roll(s, span, axis=0)
                span *= 2
            s = s[POOL_HALO:, :]
            head_count = jnp.minimum(head_pos, w).astype(jnp.float32)
            mean = jnp.concatenate(
                [s[:head, :] / head_count, s[head:, :] * (1.0 / w)], axis=0)
            y_b.append(mean - ue[POOL_HALO:, :])
        ubuf[nxt, 0:POOL_HALO, :] = ubuf[k, t_sub:, :]
        y = jnp.dot(jnp.concatenate([y_a] + y_b, axis=-1).astype(jnp.bfloat16), wcat[...],
                    preferred_element_type=jnp.float32)
        o_ref[0, rows, :] = x + y


def _mixer(x, gain, w_in, conv_w, pool_w, pool_scale, w_out, side):
    b, s, d_model = x.shape
    d_conv = conv_w.shape[1] // CONV_WIDTH
    d_pool = pool_scale.shape[1]
    assert s % MIXER_TOKEN_TILE == 0 and MIXER_TOKEN_TILE % MIXER_SUB_TILES == 0
    t_sub = MIXER_TOKEN_TILE // MIXER_SUB_TILES
    assert t_sub % V7X_SUBLANES == 0 and t_sub >= POOL_HALO
    grid = (b, s // MIXER_TOKEN_TILE)
    tile = pl.BlockSpec((1, MIXER_TOKEN_TILE, d_model), lambda i, j: (i, j, 0))
    side_specs, side_shapes = _side_specs(side, grid)
    outs = pl.pallas_call(
        functools.partial(_mixer_kernel, n_side=len(side)),
        grid=grid,
        in_specs=[
            tile,
            _resident((1, d_model)),
            _resident(w_in.shape),
            _resident(conv_w.shape),
            _resident(pool_w.shape),
            _resident(pool_scale.shape),
            _resident(w_out.shape),
        ] + side_specs,
        out_specs=[tile] + side_specs,
        out_shape=[jax.ShapeDtypeStruct(x.shape, x.dtype)] + side_shapes,
        scratch_shapes=[
            pltpu.VMEM((MIXER_SUB_TILES, CONV_HALO + t_sub, d_conv), jnp.float32),
            pltpu.VMEM((MIXER_SUB_TILES, POOL_HALO + t_sub, d_pool), jnp.float32),
            pltpu.VMEM(w_out.shape, jnp.bfloat16),
        ],
        compiler_params=pltpu.CompilerParams(
            dimension_semantics=("arbitrary", "arbitrary"),
            vmem_limit_bytes=VMEM_LIMIT_BYTES),
        name="mixer",
    )(x, gain, w_in, conv_w, pool_w, pool_scale, w_out, *side)
    return outs[0], list(outs[1:])


def kernel(x, norm_ffn1, ffn1_w_gate, ffn1_w_up, ffn1_w_down, norm_mix, w_in, conv_w,
           pool_w, pool_scale, w_out, norm_ffn2, ffn2_w_gate, ffn2_w_up, ffn2_w_down,
           norm_final):
    b, s, d_model = x.shape
    depth = norm_ffn1.shape[0]
    assert depth >= 1
    final_gain = norm_final.reshape(1, d_model)
    ffn1_w = [ffn1_w_gate[0], ffn1_w_up[0], ffn1_w_down[0]]
    for l in range(depth):
        last = l == depth - 1
        mix_w = [w_in[l], w_out[l]]
        ffn2_w = [ffn2_w_gate[l], ffn2_w_up[l], ffn2_w_down[l]]
        next_w = [] if last else [ffn1_w_gate[l + 1], ffn1_w_up[l + 1], ffn1_w_down[l + 1]]

        x2d, later_w = _ffn(x.reshape(b * s, d_model), norm_ffn1[l].reshape(1, d_model),
                            *ffn1_w, final_gain, mix_w + ffn2_w, final_norm=False)
        mix_w, ffn2_w = later_w[:2], later_w[2:]
        x, _ = _mixer(x2d.reshape(b, s, d_model), norm_mix[l].reshape(1, d_model),
                      mix_w[0], conv_w[l].reshape(1, -1), pool_w[l],
                      pool_scale[l].reshape(1, -1), mix_w[1], [])
        x2d, ffn1_w = _ffn(x.reshape(b * s, d_model), norm_ffn2[l].reshape(1, d_model),
                           *ffn2_w, final_gain, next_w, final_norm=last)
        x = x2d.reshape(b, s, d_model)
    return x
```

```python
import functools

import jax
import jax.numpy as jnp
from jax import lax
from jax.experimental import pallas as pl
from jax.experimental.pallas import tpu as pltpu

EPS = 1e-6
CONV_WIDTH = 3
POOL_WINDOWS = (2, 4, 8, 16)

V7X_SUBLANES = 8
V7X_BF16_ROWS = 16
V7X_VMEM_BYTES = 64 * 1024 * 1024

FFN_TOKEN_TILE = 1024
FFN_SUB_TILES = 4
MIXER_TOKEN_TILE = 1024
MIXER_SUB_TILES = 2
CONV_HALO = V7X_SUBLANES
POOL_HALO = 2 * V7X_SUBLANES
LOAD_CHUNKS = 16
LOAD_DEPTH = 7
VMEM_LIMIT_BYTES = V7X_VMEM_BYTES * 7 // 8


def _rms_norm(x, gain):
    y = x * lax.rsqrt(jnp.mean(x * x, axis=-1, keepdims=True) + EPS)
    return y * gain


def _resident(shape):
    return pl.BlockSpec(shape, lambda *_: (0,) * len(shape), pipeline_mode=pl.Buffered(1))


def _side_specs(weights, grid):
    n_steps = 1
    for g in grid:
        n_steps *= g
    specs, shapes = [], []
    for w in weights:
        rows, cols = w.shape
        hold = 1
        while (rows * hold) % n_steps or (rows * hold // n_steps) % V7X_BF16_ROWS:
            hold *= 2
            assert hold <= n_steps, (rows, n_steps)

        def index_map(*idx, hold=hold):
            step = idx[0]
            for g, i in zip(grid[1:], idx[1:]):
                step = step * g + i
            return (step // hold, 0)

        specs.append(pl.BlockSpec((rows * hold // n_steps, cols), index_map))
        shapes.append(jax.ShapeDtypeStruct(w.shape, jnp.bfloat16))
    return specs, shapes


def _side_cast(src_refs, dst_refs):
    for src, dst in zip(src_refs, dst_refs):
        dst[...] = src[...].astype(jnp.bfloat16)


def _load_cast(jobs):
    chunks = []
    ring_pos = {}
    for w_hbm, w_vmem, stage, sem in jobs:
        rows = stage.shape[1]
        for c in range(w_hbm.shape[0] // rows):
            slot = ring_pos.get(id(stage), 0)
            ring_pos[id(stage)] = (slot + 1) % LOAD_DEPTH
            chunks.append((w_hbm, w_vmem, stage, sem, c * rows, rows, slot))

    def copy(k):
        w_hbm, _, stage, sem, row0, rows, slot = chunks[k]
        return pltpu.make_async_copy(w_hbm.at[pl.ds(row0, rows)], stage.at[slot], sem.at[slot])

    for k in range(min(LOAD_DEPTH, len(chunks))):
        copy(k).start()
    for k, (_, w_vmem, stage, _, row0, rows, slot) in enumerate(chunks):
        copy(k).wait()
        w_vmem[pl.ds(row0, rows), :] = stage[slot].astype(jnp.bfloat16)
        if k + LOAD_DEPTH < len(chunks):
            copy(k + LOAD_DEPTH).start()


def _ffn_kernel(*refs, n_side, final_norm, cast_own):
    x_ref, g_ref, wg_ref, wu_ref, wd_ref, gf_ref = refs[:6]
    side_in = refs[6:6 + n_side]
    o_ref = refs[6 + n_side]
    side_out = refs[7 + n_side:7 + 2 * n_side]
    scratch = refs[7 + 2 * n_side:]

    if cast_own:
        wg_bf, wu_bf, wd_bf, stage_in, stage_out, sem_in, sem_out = scratch

        @pl.when(pl.program_id(0) == 0)
        def _():
            _load_cast([(wg_ref, wg_bf, stage_in, sem_in),
                        (wu_ref, wu_bf, stage_in, sem_in),
                        (wd_ref, wd_bf, stage_out, sem_out)])

        wg_ref, wu_ref, wd_ref = wg_bf, wu_bf, wd_bf

    _side_cast(side_in, side_out)
    t_sub = x_ref.shape[0] // FFN_SUB_TILES
    acts = []
    for k in range(FFN_SUB_TILES):
        rows = pl.ds(k * t_sub, t_sub)
        h = _rms_norm(x_ref[rows, :], g_ref[...]).astype(jnp.bfloat16)
        gate = jnp.dot(h, wg_ref[...], preferred_element_type=jnp.float32)
        up = jnp.dot(h, wu_ref[...], preferred_element_type=jnp.float32)
        acts.append((gate * jax.nn.sigmoid(gate) * up).astype(jnp.bfloat16))
    for k in range(FFN_SUB_TILES):
        rows = pl.ds(k * t_sub, t_sub)
        y = jnp.dot(acts[k], wd_ref[...], preferred_element_type=jnp.float32)
        out = x_ref[rows, :] + 0.5 * y
        if final_norm:
            out = _rms_norm(out, gf_ref[...])
        o_ref[rows, :] = out


def _ffn(x2d, gain, w_gate, w_up, w_down, final_gain, side, *, final_norm):
    n_tok, d_model = x2d.shape
    d_ff = w_gate.shape[1]
    assert n_tok % FFN_TOKEN_TILE == 0
    grid = (n_tok // FFN_TOKEN_TILE,)
    cast_own = w_gate.dtype == jnp.float32
    assert w_up.dtype == w_gate.dtype and w_down.dtype == w_gate.dtype
    tile = pl.BlockSpec((FFN_TOKEN_TILE, d_model), lambda i: (i, 0))
    side_specs, side_shapes = _side_specs(side, grid)
    if cast_own:
        assert d_model % (LOAD_CHUNKS * V7X_BF16_ROWS) == 0
        assert d_ff % (LOAD_CHUNKS * V7X_BF16_ROWS) == 0
        hbm = pl.BlockSpec(memory_space=pl.ANY)
        w_specs = [hbm, hbm, hbm]
        scratch = [
            pltpu.VMEM((d_model, d_ff), jnp.bfloat16),
            pltpu.VMEM((d_model, d_ff), jnp.bfloat16),
            pltpu.VMEM((d_ff, d_model), jnp.bfloat16),
            pltpu.VMEM((LOAD_DEPTH, d_model // LOAD_CHUNKS, d_ff), jnp.float32),
            pltpu.VMEM((LOAD_DEPTH, d_ff // LOAD_CHUNKS, d_model), jnp.float32),
            pltpu.SemaphoreType.DMA((LOAD_DEPTH,)),
            pltpu.SemaphoreType.DMA((LOAD_DEPTH,)),
        ]
    else:
        w_specs = [_resident(w_gate.shape), _resident(w_up.shape), _resident(w_down.shape)]
        scratch = []
    outs = pl.pallas_call(
        functools.partial(_ffn_kernel, n_side=len(side), final_norm=final_norm,
                          cast_own=cast_own),
        grid=grid,
        in_specs=[tile, _resident((1, d_model))] + w_specs + [_resident((1, d_model))]
        + side_specs,
        out_specs=[tile] + side_specs,
        out_shape=[jax.ShapeDtypeStruct(x2d.shape, x2d.dtype)] + side_shapes,
        scratch_shapes=scratch,
        compiler_params=pltpu.CompilerParams(
            dimension_semantics=("arbitrary",), vmem_limit_bytes=VMEM_LIMIT_BYTES),
        name="ffn_final" if final_norm else "ffn",
    )(x2d, gain, w_gate, w_up, w_down, final_gain, *side)
    return outs[0], list(outs[1:])


def _mixer_kernel(*refs, n_side):
    x_ref, g_ref, win_ref, cw_ref, pw_ref, ps_ref, wout_ref = refs[:7]
    side_in = refs[7:7 + n_side]
    o_ref = refs[7 + n_side]
    side_out = refs[8 + n_side:8 + 2 * n_side]
    zbuf, ubuf, wcat = refs[8 + 2 * n_side:]
    j = pl.program_id(1)
    t_tile = x_ref.shape[1]
    n_sub = zbuf.shape[0]
    t_sub = t_tile // n_sub
    d_conv = zbuf.shape[2]
    d_pool = ubuf.shape[2]
    n_groups = len(POOL_WINDOWS)
    gc = d_pool // n_groups

    @pl.when((pl.program_id(0) == 0) & (j == 0))
    def _():
        wcat[0:d_conv, :] = wout_ref[0:d_conv, :]
        for g in range(n_groups):
            lo = g * gc
            p_g = (pw_ref[g] * ps_ref[:, lo:lo + gc]).astype(jnp.bfloat16)
            wcat[d_conv + lo:d_conv + lo + gc, :] = jnp.dot(
                p_g, wout_ref[d_conv + lo:d_conv + lo + gc, :],
                preferred_element_type=jnp.float32).astype(jnp.bfloat16)

    @pl.when(j == 0)
    def _():
        zbuf[0] = jnp.zeros((CONV_HALO, d_conv), jnp.float32)
        ubuf[0] = jnp.zeros((POOL_HALO, d_pool), jnp.float32)

    _side_cast(side_in, side_out)

    for k in range(n_sub):
        rows = pl.ds(k * t_sub, t_sub)
        nxt = (k + 1) % n_sub
        x = x_ref[0, rows, :]
        h = _rms_norm(x, g_ref[...]).astype(jnp.bfloat16)

        proj = jnp.dot(h, win_ref[...], preferred_element_type=jnp.float32)
        v = proj[:, 0:d_conv]
        gate_b = proj[:, d_conv:2 * d_conv]
        gate_c = proj[:, 2 * d_conv:3 * d_conv]
        u = proj[:, 3 * d_conv:]
        z = gate_c * v
        ze = jnp.concatenate([zbuf[k], z], axis=0)
        conv = cw_ref[:, (CONV_WIDTH - 1) * d_conv:] * ze
        for tap in range(1, CONV_WIDTH):
            conv = conv + (cw_ref[:, (CONV_WIDTH - 1 - tap) * d_conv:(CONV_WIDTH - tap) * d_conv]
                           * pltpu.roll(ze, tap, axis=0))
        y_a = gate_b * conv[CONV_HALO:, :]
        zbuf[nxt] = z[t_sub - CONV_HALO:, :]

        head = POOL_HALO
        head_pos = (j * t_tile + k * t_sub + 1
                    + lax.broadcasted_iota(jnp.int32, (head, gc), 0))
        y_b = []
        for g, w in enumerate(POOL_WINDOWS):
            ue = jnp.concatenate([ubuf[k, :, g * gc:(g + 1) * gc], u[:, g * gc:(g + 1) * gc]],
                                 axis=0)
            s = ue
            span = 1
            while span < w:
                s = s + pltpu.roll(s, span, axis=0)
                span *= 2
            s = s[POOL_HALO:, :]
            head_count = jnp.minimum(head_pos, w).astype(jnp.float32)
            mean = jnp.concatenate(
                [s[:head, :] / head_count, s[head:, :] * (1.0 / w)], axis=0)
            y_b.append(mean - ue[POOL_HALO:, :])
        ubuf[nxt] = u[t_sub - POOL_HALO:, :]
        y = jnp.dot(jnp.concatenate([y_a] + y_b, axis=-1).astype(jnp.bfloat16), wcat[...],
                    preferred_element_type=jnp.float32)
        o_ref[0, rows, :] = x + y


def _mixer(x, gain, w_in, conv_w, pool_w, pool_scale, w_out, side):
    b, s, d_model = x.shape
    d_conv = conv_w.shape[1] // CONV_WIDTH
    d_pool = pool_scale.shape[1]
    assert s % MIXER_TOKEN_TILE == 0 and MIXER_TOKEN_TILE % MIXER_SUB_TILES == 0
    t_sub = MIXER_TOKEN_TILE // MIXER_SUB_TILES
    assert t_sub % V7X_SUBLANES == 0 and t_sub >= POOL_HALO
    grid = (b, s // MIXER_TOKEN_TILE)
    tile = pl.BlockSpec((1, MIXER_TOKEN_TILE, d_model), lambda i, j: (i, j, 0))
    side_specs, side_shapes = _side_specs(side, grid)
    outs = pl.pallas_call(
        functools.partial(_mixer_kernel, n_side=len(side)),
        grid=grid,
        in_specs=[
            tile,
            _resident((1, d_model)),
            _resident(w_in.shape),
            _resident(conv_w.shape),
            _resident(pool_w.shape),
            _resident(pool_scale.shape),
            _resident(w_out.shape),
        ] + side_specs,
        out_specs=[tile] + side_specs,
        out_shape=[jax.ShapeDtypeStruct(x.shape, x.dtype)] + side_shapes,
        scratch_shapes=[
            pltpu.VMEM((MIXER_SUB_TILES, CONV_HALO, d_conv), jnp.float32),
            pltpu.VMEM((MIXER_SUB_TILES, POOL_HALO, d_pool), jnp.float32),
            pltpu.VMEM(w_out.shape, jnp.bfloat16),
        ],
        compiler_params=pltpu.CompilerParams(
            dimension_semantics=("arbitrary", "arbitrary"),
            vmem_limit_bytes=VMEM_LIMIT_BYTES),
        name="mixer",
    )(x, gain, w_in, conv_w, pool_w, pool_scale, w_out, *side)
    return outs[0], list(outs[1:])


def kernel(x, norm_ffn1, ffn1_w_gate, ffn1_w_up, ffn1_w_down, norm_mix, w_in, conv_w,
           pool_w, pool_scale, w_out, norm_ffn2, ffn2_w_gate, ffn2_w_up, ffn2_w_down,
           norm_final):
    b, s, d_model = x.shape
    depth = norm_ffn1.shape[0]
    assert depth >= 1
    final_gain = norm_final.reshape(1, d_model)
    ffn1_w = [ffn1_w_gate[0], ffn1_w_up[0], ffn1_w_down[0]]
    for l in range(depth):
        last = l == depth - 1
        mix_w = [w_in[l], w_out[l]]
        ffn2_w = [ffn2_w_gate[l], ffn2_w_up[l], ffn2_w_down[l]]
        next_w = [] if last else [ffn1_w_gate[l + 1], ffn1_w_up[l + 1], ffn1_w_down[l + 1]]

        x2d, mix_w = _ffn(x.reshape(b * s, d_model), norm_ffn1[l].reshape(1, d_model),
                          *ffn1_w, final_gain, mix_w, final_norm=False)
        x, ffn2_w = _mixer(x2d.reshape(b, s, d_model), norm_mix[l].reshape(1, d_model),
                           mix_w[0], conv_w[l].reshape(1, -1), pool_w[l],
                           pool_scale[l].reshape(1, -1),
                           mix_w[1], ffn2_w)
        x2d, ffn1_w = _ffn(x.reshape(b * s, d_model), norm_ffn2[l].reshape(1, d_model),
                           *ffn2_w, final_gain, next_w, final_norm=last)
        x = x2d.reshape(b, s, d_model)
    return x
```

```python
import functools

import jax
import jax.numpy as jnp
from jax import lax
from jax.experimental import pallas as pl
from jax.experimental.pallas import tpu as pltpu

EPS = 1e-6
CONV_WIDTH = 3
POOL_WINDOWS = (2, 4, 8, 16)

V7X_SUBLANES = 8
V7X_BF16_ROWS = 16
V7X_VMEM_BYTES = 64 * 1024 * 1024

FFN_TOKEN_TILE = 1024
FFN_SUB_TILES = 4
MIXER_TOKEN_TILE = 1024
MIXER_SUB_TILES = 4
CONV_HALO = V7X_SUBLANES
POOL_HALO = 2 * V7X_SUBLANES
LOAD_CHUNKS = 16
LOAD_DEPTH = 7
VMEM_LIMIT_BYTES = V7X_VMEM_BYTES * 7 // 8


def _rms_norm(x, gain):
    y = x * lax.rsqrt(jnp.mean(x * x, axis=-1, keepdims=True) + EPS)
    return y * gain


def _resident(shape):
    return pl.BlockSpec(shape, lambda *_: (0,) * len(shape), pipeline_mode=pl.Buffered(1))


def _side_specs(weights, grid):
    n_steps = 1
    for g in grid:
        n_steps *= g
    specs, shapes = [], []
    for w in weights:
        rows, cols = w.shape
        hold = 1
        while (rows * hold) % n_steps or (rows * hold // n_steps) % V7X_BF16_ROWS:
            hold *= 2
            assert hold <= n_steps, (rows, n_steps)

        def index_map(*idx, hold=hold):
            step = idx[0]
            for g, i in zip(grid[1:], idx[1:]):
                step = step * g + i
            return (step // hold, 0)

        specs.append(pl.BlockSpec((rows * hold // n_steps, cols), index_map))
        shapes.append(jax.ShapeDtypeStruct(w.shape, jnp.bfloat16))
    return specs, shapes


def _side_cast(src_refs, dst_refs):
    for src, dst in zip(src_refs, dst_refs):
        dst[...] = src[...].astype(jnp.bfloat16)


def _load_cast(jobs):
    chunks = []
    ring_pos = {}
    for w_hbm, w_vmem, stage, sem in jobs:
        rows = stage.shape[1]
        for c in range(w_hbm.shape[0] // rows):
            slot = ring_pos.get(id(stage), 0)
            ring_pos[id(stage)] = (slot + 1) % LOAD_DEPTH
            chunks.append((w_hbm, w_vmem, stage, sem, c * rows, rows, slot))

    def copy(k):
        w_hbm, _, stage, sem, row0, rows, slot = chunks[k]
        return pltpu.make_async_copy(w_hbm.at[pl.ds(row0, rows)], stage.at[slot], sem.at[slot])

    for k in range(min(LOAD_DEPTH, len(chunks))):
        copy(k).start()
    for k, (_, w_vmem, stage, _, row0, rows, slot) in enumerate(chunks):
        copy(k).wait()
        w_vmem[pl.ds(row0, rows), :] = stage[slot].astype(jnp.bfloat16)
        if k + LOAD_DEPTH < len(chunks):
            copy(k + LOAD_DEPTH).start()


def _ffn_kernel(*refs, n_side, final_norm, cast_own):
    x_ref, g_ref, wg_ref, wu_ref, wd_ref, gf_ref = refs[:6]
    side_in = refs[6:6 + n_side]
    o_ref = refs[6 + n_side]
    side_out = refs[7 + n_side:7 + 2 * n_side]
    scratch = refs[7 + 2 * n_side:]

    if cast_own:
        wg_bf, wu_bf, wd_bf, stage_in, stage_out, sem_in, sem_out = scratch

        @pl.when(pl.program_id(0) == 0)
        def _():
            _load_cast([(wg_ref, wg_bf, stage_in, sem_in),
                        (wu_ref, wu_bf, stage_in, sem_in),
                        (wd_ref, wd_bf, stage_out, sem_out)])

        wg_ref, wu_ref, wd_ref = wg_bf, wu_bf, wd_bf

    _side_cast(side_in, side_out)
    t_sub = x_ref.shape[0] // FFN_SUB_TILES
    acts = []
    for k in range(FFN_SUB_TILES):
        rows = pl.ds(k * t_sub, t_sub)
        h = _rms_norm(x_ref[rows, :], g_ref[...]).astype(jnp.bfloat16)
        gate = jnp.dot(h, wg_ref[...], preferred_element_type=jnp.float32)
        up = jnp.dot(h, wu_ref[...], preferred_element_type=jnp.float32)
        acts.append((gate * jax.nn.sigmoid(gate) * up).astype(jnp.bfloat16))
    for k in range(FFN_SUB_TILES):
        rows = pl.ds(k * t_sub, t_sub)
        y = jnp.dot(acts[k], wd_ref[...], preferred_element_type=jnp.float32)
        out = x_ref[rows, :] + 0.5 * y
        if final_norm:
            out = _rms_norm(out, gf_ref[...])
        o_ref[rows, :] = out


def _ffn(x2d, gain, w_gate, w_up, w_down, final_gain, side, *, final_norm):
    n_tok, d_model = x2d.shape
    d_ff = w_gate.shape[1]
    assert n_tok % FFN_TOKEN_TILE == 0
    grid = (n_tok // FFN_TOKEN_TILE,)
    cast_own = w_gate.dtype == jnp.float32
    assert w_up.dtype == w_gate.dtype and w_down.dtype == w_gate.dtype
    tile = pl.BlockSpec((FFN_TOKEN_TILE, d_model), lambda i: (i, 0))
    side_specs, side_shapes = _side_specs(side, grid)
    if cast_own:
        assert d_model % (LOAD_CHUNKS * V7X_BF16_ROWS) == 0
        assert d_ff % (LOAD_CHUNKS * V7X_BF16_ROWS) == 0
        hbm = pl.BlockSpec(memory_space=pl.ANY)
        w_specs = [hbm, hbm, hbm]
        scratch = [
            pltpu.VMEM((d_model, d_ff), jnp.bfloat16),
            pltpu.VMEM((d_model, d_ff), jnp.bfloat16),
            pltpu.VMEM((d_ff, d_model), jnp.bfloat16),
            pltpu.VMEM((LOAD_DEPTH, d_model // LOAD_CHUNKS, d_ff), jnp.float32),
            pltpu.VMEM((LOAD_DEPTH, d_ff // LOAD_CHUNKS, d_model), jnp.float32),
            pltpu.SemaphoreType.DMA((LOAD_DEPTH,)),
            pltpu.SemaphoreType.DMA((LOAD_DEPTH,)),
        ]
    else:
        w_specs = [_resident(w_gate.shape), _resident(w_up.shape), _resident(w_down.shape)]
        scratch = []
    outs = pl.pallas_call(
        functools.partial(_ffn_kernel, n_side=len(side), final_norm=final_norm,
                          cast_own=cast_own),
        grid=grid,
        in_specs=[tile, _resident((1, d_model))] + w_specs + [_resident((1, d_model))]
        + side_specs,
        out_specs=[tile] + side_specs,
        out_shape=[jax.ShapeDtypeStruct(x2d.shape, x2d.dtype)] + side_shapes,
        scratch_shapes=scratch,
        compiler_params=pltpu.CompilerParams(
            dimension_semantics=("arbitrary",), vmem_limit_bytes=VMEM_LIMIT_BYTES),
        name="ffn_final" if final_norm else "ffn",
    )(x2d, gain, w_gate, w_up, w_down, final_gain, *side)
    return outs[0], list(outs[1:])


def _mixer_kernel(*refs, n_side):
    x_ref, g_ref, win_ref, cw_ref, pw_ref, ps_ref, wout_ref = refs[:7]
    side_in = refs[7:7 + n_side]
    o_ref = refs[7 + n_side]
    side_out = refs[8 + n_side:8 + 2 * n_side]
    zbuf, ubuf, wcat = refs[8 + 2 * n_side:]
    j = pl.program_id(1)
    t_tile = x_ref.shape[1]
    n_sub = zbuf.shape[0]
    t_sub = t_tile // n_sub
    d_conv = zbuf.shape[2]
    d_pool = ubuf.shape[2]
    n_groups = len(POOL_WINDOWS)
    gc = d_pool // n_groups

    @pl.when((pl.program_id(0) == 0) & (j == 0))
    def _():
        wcat[0:d_conv, :] = wout_ref[0:d_conv, :]
        for g in range(n_groups):
            lo = g * gc
            p_g = (pw_ref[g] * ps_ref[:, lo:lo + gc]).astype(jnp.bfloat16)
            wcat[d_conv + lo:d_conv + lo + gc, :] = jnp.dot(
                p_g, wout_ref[d_conv + lo:d_conv + lo + gc, :],
                preferred_element_type=jnp.float32).astype(jnp.bfloat16)

    @pl.when(j == 0)
    def _():
        zbuf[0, 0:CONV_HALO, :] = jnp.zeros((CONV_HALO, d_conv), jnp.float32)
        ubuf[0, 0:POOL_HALO, :] = jnp.zeros((POOL_HALO, d_pool), jnp.float32)

    _side_cast(side_in, side_out)

    for k in range(n_sub):
        rows = pl.ds(k * t_sub, t_sub)
        nxt = (k + 1) % n_sub
        x = x_ref[0, rows, :]
        h = _rms_norm(x, g_ref[...]).astype(jnp.bfloat16)

        proj = jnp.dot(h, win_ref[...], preferred_element_type=jnp.float32)
        v = proj[:, 0:d_conv]
        gate_b = proj[:, d_conv:2 * d_conv]
        gate_c = proj[:, 2 * d_conv:3 * d_conv]
        ubuf[k, POOL_HALO:, :] = proj[:, 3 * d_conv:]
        zbuf[k, CONV_HALO:, :] = gate_c * v
        ze = zbuf[k]
        conv = cw_ref[:, (CONV_WIDTH - 1) * d_conv:] * ze
        for tap in range(1, CONV_WIDTH):
            conv = conv + (cw_ref[:, (CONV_WIDTH - 1 - tap) * d_conv:(CONV_WIDTH - tap) * d_conv]
                           * pltpu.roll(ze, tap, axis=0))
        y_a = gate_b * conv[CONV_HALO:, :]
        zbuf[nxt, 0:CONV_HALO, :] = ze[t_sub:, :]

        head = POOL_HALO
        head_pos = (j * t_tile + k * t_sub + 1
                    + lax.broadcasted_iota(jnp.int32, (head, gc), 0))
        y_b = []
        for g, w in enumerate(POOL_WINDOWS):
            ue = ubuf[k, :, g * gc:(g + 1) * gc]
            s = ue
            span = 1
            while span < w:
                s = s + pltpu.roll(s, span, axis=0)
                span *= 2
            s = s[POOL_HALO:, :]
            head_count = jnp.minimum(head_pos, w).astype(jnp.float32)
            mean = jnp.concatenate(
                [s[:head, :] / head_count, s[head:, :] * (1.0 / w)], axis=0)
            y_b.append(mean - ue[POOL_HALO:, :])
        ubuf[nxt, 0:POOL_HALO, :] = ubuf[k, t_sub:, :]
        y = jnp.dot(jnp.concatenate([y_a] + y_b, axis=-1).astype(jnp.bfloat16), wcat[...],
                    preferred_element_type=jnp.float32)
        o_ref[0, rows, :] = x + y


def _mixer(x, gain, w_in, conv_w, pool_w, pool_scale, w_out, side):
    b, s, d_model = x.shape
    d_conv = conv_w.shape[1] // CONV_WIDTH
    d_pool = pool_scale.shape[1]
    assert s % MIXER_TOKEN_TILE == 0 and MIXER_TOKEN_TILE % MIXER_SUB_TILES == 0
    t_sub = MIXER_TOKEN_TILE // MIXER_SUB_TILES
    assert t_sub % V7X_SUBLANES == 0 and t_sub >= POOL_HALO
    grid = (b, s // MIXER_TOKEN_TILE)
    tile = pl.BlockSpec((1, MIXER_TOKEN_TILE, d_model), lambda i, j: (i, j, 0))
    side_specs, side_shapes = _side_specs(side, grid)
    outs = pl.pallas_call(
        functools.partial(_mixer_kernel, n_side=len(side)),
        grid=grid,
        in_specs=[
            tile,
            _resident((1, d_model)),
            _resident(w_in.shape),
            _resident(conv_w.shape),
            _resident(pool_w.shape),
            _resident(pool_scale.shape),
            _resident(w_out.shape),
        ] + side_specs,
        out_specs=[tile] + side_specs,
        out_shape=[jax.ShapeDtypeStruct(x.shape, x.dtype)] + side_shapes,
        scratch_shapes=[
            pltpu.VMEM((MIXER_SUB_TILES, CONV_HALO + t_sub, d_conv), jnp.float32),
            pltpu.VMEM((MIXER_SUB_TILES, POOL_HALO + t_sub, d_pool), jnp.float32),
            pltpu.VMEM(w_out.shape, jnp.bfloat16),
        ],
        compiler_params=pltpu.CompilerParams(
            dimension_semantics=("arbitrary", "arbitrary"),
            vmem_limit_bytes=VMEM_LIMIT_BYTES),
        name="mixer",
    )(x, gain, w_in, conv_w, pool_w, pool_scale, w_out, *side)
    return outs[0], list(outs[1:])


def kernel(x, norm_ffn1, ffn1_w_gate, ffn1_w_up, ffn1_w_down, norm_mix, w_in, conv_w,
           pool_w, pool_scale, w_out, norm_ffn2, ffn2_w_gate, ffn2_w_up, ffn2_w_down,
           norm_final):
    b, s, d_model = x.shape
    depth = norm_ffn1.shape[0]
    assert depth >= 1
    final_gain = norm_final.reshape(1, d_model)
    ffn1_w = [ffn1_w_gate[0], ffn1_w_up[0], ffn1_w_down[0]]
    for l in range(depth):
        last = l == depth - 1
        mix_w = [w_in[l], w_out[l]]
        ffn2_w = [ffn2_w_gate[l], ffn2_w_up[l], ffn2_w_down[l]]
        next_w = [] if last else [ffn1_w_gate[l + 1], ffn1_w_up[l + 1], ffn1_w_down[l + 1]]

        x2d, mix_w = _ffn(x.reshape(b * s, d_model), norm_ffn1[l].reshape(1, d_model),
                          *ffn1_w, final_gain, mix_w, final_norm=False)
        x, ffn2_w = _mixer(x2d.reshape(b, s, d_model), norm_mix[l].reshape(1, d_model),
                           mix_w[0], conv_w[l].reshape(1, -1), pool_w[l],
                           pool_scale[l].reshape(1, -1),
                           mix_w[1], ffn2_w)
        x2d, ffn1_w = _ffn(x.reshape(b * s, d_model), norm_ffn2[l].reshape(1, d_model),
                           *ffn2_w, final_gain, next_w, final_norm=last)
        x = x2d.reshape(b, s, d_model)
    return x
```

```python
import functools

import jax
import jax.numpy as jnp
from jax import lax
from jax.experimental import pallas as pl
from jax.experimental.pallas import tpu as pltpu

EPS = 1e-6
CONV_WIDTH = 3
POOL_WINDOWS = (2, 4, 8, 16)

V7X_SUBLANES = 8
V7X_BF16_ROWS = 16
V7X_VMEM_BYTES = 64 * 1024 * 1024

FFN_TOKEN_TILE = 1024
FFN_SUB_TILES = 4
MIXER_TOKEN_TILE = 1024
MIXER_SUB_TILES = 2
CONV_HALO = V7X_SUBLANES
POOL_HALO = 2 * V7X_SUBLANES
LOAD_CHUNKS = 16
LOAD_DEPTH = 7
VMEM_LIMIT_BYTES = V7X_VMEM_BYTES * 7 // 8


def _rms_norm(x, gain):
    y = x * lax.rsqrt(jnp.mean(x * x, axis=-1, keepdims=True) + EPS)
    return y * gain


def _resident(shape):
    return pl.BlockSpec(shape, lambda *_: (0,) * len(shape), pipeline_mode=pl.Buffered(1))


def _side_specs(weights, grid):
    n_steps = 1
    for g in grid:
        n_steps *= g
    specs, shapes = [], []
    for w in weights:
        rows, cols = w.shape
        hold = 1
        while (rows * hold) % n_steps or (rows * hold // n_steps) % V7X_BF16_ROWS:
            hold *= 2
            assert hold <= n_steps, (rows, n_steps)

        def index_map(*idx, hold=hold):
            step = idx[0]
            for g, i in zip(grid[1:], idx[1:]):
                step = step * g + i
            return (step // hold, 0)

        specs.append(pl.BlockSpec((rows * hold // n_steps, cols), index_map))
        shapes.append(jax.ShapeDtypeStruct(w.shape, jnp.bfloat16))
    return specs, shapes


def _side_cast(src_refs, dst_refs):
    for src, dst in zip(src_refs, dst_refs):
        dst[...] = src[...].astype(jnp.bfloat16)


def _load_cast(jobs):
    chunks = []
    ring_pos = {}
    for w_hbm, w_vmem, stage, sem in jobs:
        rows = stage.shape[1]
        for c in range(w_hbm.shape[0] // rows):
            slot = ring_pos.get(id(stage), 0)
            ring_pos[id(stage)] = (slot + 1) % LOAD_DEPTH
            chunks.append((w_hbm, w_vmem, stage, sem, c * rows, rows, slot))

    def copy(k):
        w_hbm, _, stage, sem, row0, rows, slot = chunks[k]
        return pltpu.make_async_copy(w_hbm.at[pl.ds(row0, rows)], stage.at[slot], sem.at[slot])

    for k in range(min(LOAD_DEPTH, len(chunks))):
        copy(k).start()
    for k, (_, w_vmem, stage, _, row0, rows, slot) in enumerate(chunks):
        copy(k).wait()
        w_vmem[pl.ds(row0, rows), :] = stage[slot].astype(jnp.bfloat16)
        if k + LOAD_DEPTH < len(chunks):
            copy(k + LOAD_DEPTH).start()


def _ffn_kernel(*refs, n_side, final_norm, cast_own):
    x_ref, g_ref, wg_ref, wu_ref, wd_ref, gf_ref = refs[:6]
    side_in = refs[6:6 + n_side]
    o_ref = refs[6 + n_side]
    side_out = refs[7 + n_side:7 + 2 * n_side]
    scratch = refs[7 + 2 * n_side:]

    if cast_own:
        wg_bf, wu_bf, wd_bf, stage_in, stage_out, sem_in, sem_out = scratch

        @pl.when(pl.program_id(0) == 0)
        def _():
            _load_cast([(wg_ref, wg_bf, stage_in, sem_in),
                        (wu_ref, wu_bf, stage_in, sem_in),
                        (wd_ref, wd_bf, stage_out, sem_out)])

        wg_ref, wu_ref, wd_ref = wg_bf, wu_bf, wd_bf

    _side_cast(side_in, side_out)
    t_sub = x_ref.shape[0] // FFN_SUB_TILES
    acts = []
    for k in range(FFN_SUB_TILES):
        rows = pl.ds(k * t_sub, t_sub)
        h = _rms_norm(x_ref[rows, :], g_ref[...]).astype(jnp.bfloat16)
        gate = jnp.dot(h, wg_ref[...], preferred_element_type=jnp.float32)
        up = jnp.dot(h, wu_ref[...], preferred_element_type=jnp.float32)
        acts.append((gate * jax.nn.sigmoid(gate) * up).astype(jnp.bfloat16))
    for k in range(FFN_SUB_TILES):
        rows = pl.ds(k * t_sub, t_sub)
        y = jnp.dot(acts[k], wd_ref[...], preferred_element_type=jnp.float32)
        out = x_ref[rows, :] + 0.5 * y
        if final_norm:
            out = _rms_norm(out, gf_ref[...])
        o_ref[rows, :] = out


def _ffn(x2d, gain, w_gate, w_up, w_down, final_gain, side, *, final_norm):
    n_tok, d_model = x2d.shape
    d_ff = w_gate.shape[1]
    assert n_tok % FFN_TOKEN_TILE == 0
    grid = (n_tok // FFN_TOKEN_TILE,)
    cast_own = w_gate.dtype == jnp.float32
    assert w_up.dtype == w_gate.dtype and w_down.dtype == w_gate.dtype
    tile = pl.BlockSpec((FFN_TOKEN_TILE, d_model), lambda i: (i, 0))
    side_specs, side_shapes = _side_specs(side, grid)
    if cast_own:
        assert d_model % (LOAD_CHUNKS * V7X_BF16_ROWS) == 0
        assert d_ff % (LOAD_CHUNKS * V7X_BF16_ROWS) == 0
        hbm = pl.BlockSpec(memory_space=pl.ANY)
        w_specs = [hbm, hbm, hbm]
        scratch = [
            pltpu.VMEM((d_model, d_ff), jnp.bfloat16),
            pltpu.VMEM((d_model, d_ff), jnp.bfloat16),
            pltpu.VMEM((d_ff, d_model), jnp.bfloat16),
            pltpu.VMEM((LOAD_DEPTH, d_model // LOAD_CHUNKS, d_ff), jnp.float32),
            pltpu.VMEM((LOAD_DEPTH, d_ff // LOAD_CHUNKS, d_model), jnp.float32),
            pltpu.SemaphoreType.DMA((LOAD_DEPTH,)),
            pltpu.SemaphoreType.DMA((LOAD_DEPTH,)),
        ]
    else:
        w_specs = [_resident(w_gate.shape), _resident(w_up.shape), _resident(w_down.shape)]
        scratch = []
    outs = pl.pallas_call(
        functools.partial(_ffn_kernel, n_side=len(side), final_norm=final_norm,
                          cast_own=cast_own),
        grid=grid,
        in_specs=[tile, _resident((1, d_model))] + w_specs + [_resident((1, d_model))]
        + side_specs,
        out_specs=[tile] + side_specs,
        out_shape=[jax.ShapeDtypeStruct(x2d.shape, x2d.dtype)] + side_shapes,
        scratch_shapes=scratch,
        compiler_params=pltpu.CompilerParams(
            dimension_semantics=("arbitrary",), vmem_limit_bytes=VMEM_LIMIT_BYTES),
        name="ffn_final" if final_norm else "ffn",
    )(x2d, gain, w_gate, w_up, w_down, final_gain, *side)
    return outs[0], list(outs[1:])


def _mixer_kernel(*refs, n_side):
    x_ref, g_ref, win_ref, cw_ref, pw_ref, ps_ref, wout_ref = refs[:7]
    side_in = refs[7:7 + n_side]
    o_ref = refs[7 + n_side]
    side_out = refs[8 + n_side:8 + 2 * n_side]
    zbuf, ubuf, wcat = refs[8 + 2 * n_side:]
    j = pl.program_id(1)
    t_tile = x_ref.shape[1]
    n_sub = zbuf.shape[0]
    t_sub = t_tile // n_sub
    d_conv = zbuf.shape[2]
    d_pool = ubuf.shape[2]
    n_groups = len(POOL_WINDOWS)
    gc = d_pool // n_groups

    @pl.when((pl.program_id(0) == 0) & (j == 0))
    def _():
        wcat[0:d_conv, :] = wout_ref[0:d_conv, :]
        for g in range(n_groups):
            lo = g * gc
            p_g = (pw_ref[g] * ps_ref[:, lo:lo + gc]).astype(jnp.bfloat16)
            wcat[d_conv + lo:d_conv + lo + gc, :] = jnp.dot(
                p_g, wout_ref[d_conv + lo:d_conv + lo + gc, :],
                preferred_element_type=jnp.float32).astype(jnp.bfloat16)

    @pl.when(j == 0)
    def _():
        zbuf[0, 0:CONV_HALO, :] = jnp.zeros((CONV_HALO, d_conv), jnp.float32)
        ubuf[0, 0:POOL_HALO, :] = jnp.zeros((POOL_HALO, d_pool), jnp.float32)

    _side_cast(side_in, side_out)

    mixed = []
    for k in range(n_sub):
        rows = pl.ds(k * t_sub, t_sub)
        nxt = (k + 1) % n_sub
        x = x_ref[0, rows, :]
        h = _rms_norm(x, g_ref[...]).astype(jnp.bfloat16)

        proj = jnp.dot(h, win_ref[...], preferred_element_type=jnp.float32)
        v = proj[:, 0:d_conv]
        gate_b = proj[:, d_conv:2 * d_conv]
        gate_c = proj[:, 2 * d_conv:3 * d_conv]
        ubuf[k, POOL_HALO:, :] = proj[:, 3 * d_conv:]
        zbuf[k, CONV_HALO:, :] = gate_c * v
        ze = zbuf[k]
        conv = cw_ref[:, (CONV_WIDTH - 1) * d_conv:] * ze
        for tap in range(1, CONV_WIDTH):
            conv = conv + (cw_ref[:, (CONV_WIDTH - 1 - tap) * d_conv:(CONV_WIDTH - tap) * d_conv]
                           * pltpu.roll(ze, tap, axis=0))
        y_a = gate_b * conv[CONV_HALO:, :]
        zbuf[nxt, 0:CONV_HALO, :] = ze[t_sub:, :]

        head = POOL_HALO
        head_pos = (j * t_tile + k * t_sub + 1
                    + lax.broadcasted_iota(jnp.int32, (head, gc), 0))
        y_b = []
        for g, w in enumerate(POOL_WINDOWS):
            ue = ubuf[k, :, g * gc:(g + 1) * gc]
            s = ue
            span = 1
            while span < w:
                s = s + pltpu.roll(s, span, axis=0)
                span *= 2
            s = s[POOL_HALO:, :]
            head_count = jnp.minimum(head_pos, w).astype(jnp.float32)
            mean = jnp.concatenate(
                [s[:head, :] / head_count, s[head:, :] * (1.0 / w)], axis=0)
            y_b.append(mean - ue[POOL_HALO:, :])
        ubuf[nxt, 0:POOL_HALO, :] = ubuf[k, t_sub:, :]
        mixed.append(jnp.concatenate([y_a] + y_b, axis=-1).astype(jnp.bfloat16))

    for k in range(n_sub):
        rows = pl.ds(k * t_sub, t_sub)
        y = jnp.dot(mixed[k], wcat[...], preferred_element_type=jnp.float32)
        o_ref[0, rows, :] = x_ref[0, rows, :] + y


def _mixer(x, gain, w_in, conv_w, pool_w, pool_scale, w_out, side):
    b, s, d_model = x.shape
    d_conv = conv_w.shape[1] // CONV_WIDTH
    d_pool = pool_scale.shape[1]
    assert s % MIXER_TOKEN_TILE == 0 and MIXER_TOKEN_TILE % MIXER_SUB_TILES == 0
    t_sub = MIXER_TOKEN_TILE // MIXER_SUB_TILES
    assert t_sub % V7X_SUBLANES == 0 and t_sub >= POOL_HALO
    grid = (b, s // MIXER_TOKEN_TILE)
    tile = pl.BlockSpec((1, MIXER_TOKEN_TILE, d_model), lambda i, j: (i, j, 0))
    side_specs, side_shapes = _side_specs(side, grid)
    outs = pl.pallas_call(
        functools.partial(_mixer_kernel, n_side=len(side)),
        grid=grid,
        in_specs=[
            tile,
            _resident((1, d_model)),
            _resident(w_in.shape),
            _resident(conv_w.shape),
            _resident(pool_w.shape),
            _resident(pool_scale.shape),
            _resident(w_out.shape),
        ] + side_specs,
        out_specs=[tile] + side_specs,
        out_shape=[jax.ShapeDtypeStruct(x.shape, x.dtype)] + side_shapes,
        scratch_shapes=[
            pltpu.VMEM((MIXER_SUB_TILES, CONV_HALO + t_sub, d_conv), jnp.float32),
            pltpu.VMEM((MIXER_SUB_TILES, POOL_HALO + t_sub, d_pool), jnp.float32),
            pltpu.VMEM(w_out.shape, jnp.bfloat16),
        ],
        compiler_params=pltpu.CompilerParams(
            dimension_semantics=("arbitrary", "arbitrary"),
            vmem_limit_bytes=VMEM_LIMIT_BYTES),
        name="mixer",
    )(x, gain, w_in, conv_w, pool_w, pool_scale, w_out, *side)
    return outs[0], list(outs[1:])


def kernel(x, norm_ffn1, ffn1_w_gate, ffn1_w_up, ffn1_w_down, norm_mix, w_in, conv_w,
           pool_w, pool_scale, w_out, norm_ffn2, ffn2_w_gate, ffn2_w_up, ffn2_w_down,
           norm_final):
    b, s, d_model = x.shape
    depth = norm_ffn1.shape[0]
    assert depth >= 1
    final_gain = norm_final.reshape(1, d_model)
    ffn1_w = [ffn1_w_gate[0], ffn1_w_up[0], ffn1_w_down[0]]
    for l in range(depth):
        last = l == depth - 1
        mix_w = [w_in[l], w_out[l]]
        ffn2_w = [ffn2_w_gate[l], ffn2_w_up[l], ffn2_w_down[l]]
        next_w = [] if last else [ffn1_w_gate[l + 1], ffn1_w_up[l + 1], ffn1_w_down[l + 1]]

        x2d, mix_w = _ffn(x.reshape(b * s, d_model), norm_ffn1[l].reshape(1, d_model),
                          *ffn1_w, final_gain, mix_w, final_norm=False)
        x, ffn2_w = _mixer(x2d.reshape(b, s, d_model), norm_mix[l].reshape(1, d_model),
                           mix_w[0], conv_w[l].reshape(1, -1), pool_w[l],
                           pool_scale[l].reshape(1, -1),
                           mix_w[1], ffn2_w)
        x2d, ffn1_w = _ffn(x.reshape(b * s, d_model), norm_ffn2[l].reshape(1, d_model),
                           *ffn2_w, final_gain, next_w, final_norm=last)
        x = x2d.reshape(b, s, d_model)
    return x
```
